```python
import jax
import jax.numpy as jnp
from jax import lax
import numpy as np

D_MODEL = 2048
BATCH = 1
SEQ = 16384
DEPTH = 2

NH_M = 4
DQK_M = D_MODEL // 8
DV_M = D_MODEL // 4
NH_R = 8
DQK_R = D_MODEL // 16
DV_R = D_MODEL // 8
WM_QK = NH_M * DQK_M
WM_V = NH_M * DV_M
WR_QK = NH_R * DQK_R
WR_V = NH_R * DV_R
N_MGATE = 4 * NH_M
CONV_W = 5
CHUNK = 128
ROPE_BASE = 10000.0
EPS = 1e-6
IN_SPLITS = (WM_QK, WM_QK, WM_V, WM_V, WM_V, N_MGATE, WR_QK, WR_QK, WR_V, WR_V, D_MODEL, D_MODEL)
D_IN = sum(IN_SPLITS)

kernel_name = 'hybrid_mlstm_retention_encoder'


def rmsnorm(x, w):
    xf = x.astype(jnp.float32)
    y = xf * lax.rsqrt(jnp.mean(xf * xf, axis=-1, keepdims=True) + EPS)
    return (y * w.astype(jnp.float32)).astype(x.dtype)


def split_heads(t, n_heads):
    b, s, _ = t.shape
    return t.reshape(b, s, n_heads, -1).transpose(0, 2, 1, 3)


def head_rmsnorm(t, w):
    b, h, s, d = t.shape
    y = t * lax.rsqrt(jnp.mean(t * t, axis=-1, keepdims=True) + EPS)
    y = y * w.astype(jnp.float32).reshape(h, d)[None, :, None, :]
    return y.transpose(0, 2, 1, 3).reshape(b, s, h * d)


def to_chunks(t):
    b, h, s = t.shape[:3]
    t = t.reshape((b, h, s // CHUNK, CHUNK) + t.shape[3:])
    return jnp.moveaxis(t, 2, 0)


def from_chunks(t):
    nc, b, h, l = t.shape[:4]
    return jnp.moveaxis(t, 0, 2).reshape((b, h, nc * l) + t.shape[4:])


def flip_seq(t):
    return jnp.flip(t, axis=2)


def depthwise_conv(t, w, bias):
    c = t.shape[-1]
    out = lax.conv_general_dilated(t, w[:, None, :], window_strides=(1,), padding='SAME',
                                   dimension_numbers=('NWC', 'WIO', 'NWC'), feature_group_count=c)
    return out + bias


def rope(t, cos, sin):
    half = t.shape[-1] // 2
    t1, t2 = t[..., :half], t[..., half:]
    return jnp.concatenate([t1 * cos - t2 * sin, t1 * sin + t2 * cos], axis=-1)


def mlstm_scan(q, k, v, i_pre, f_pre):
    b, h, _, dk = q.shape
    dv = v.shape[-1]
    tril = jnp.tril(jnp.ones((CHUNK, CHUNK), dtype=bool))
    log_f = jax.nn.log_sigmoid(f_pre)

    def step(carry, inp):
        c_state, n_state, m_state = carry
        qc, kc, vc, ic, lfc = inp
        cum = jnp.cumsum(lfc, axis=-1)
        total = cum[..., -1]
        d_log = cum[..., :, None] - cum[..., None, :] + ic[..., None, :]
        d_log = jnp.where(tril, d_log, -jnp.inf)
        inter_log = cum + m_state[..., None]
        m_row = jnp.maximum(jnp.max(d_log, axis=-1), inter_log)
        w = jnp.exp(d_log - m_row[..., None]) * jnp.einsum('bhid,bhjd->bhij', qc, kc)
        a = jnp.exp(inter_log - m_row)
        num = jnp.einsum('bhij,bhjv->bhiv', w, vc) + a[..., None] * jnp.einsum('bhid,bhdv->bhiv', qc, c_state)
        den = jnp.sum(w, axis=-1) + a * jnp.einsum('bhid,bhd->bhi', qc, n_state)
        out = num / jnp.maximum(jnp.abs(den), jnp.exp(-m_row))[..., None]
        w_in = total[..., None] - cum + ic
        m_new = jnp.maximum(total + m_state, jnp.max(w_in, axis=-1))
        s_in = jnp.exp(w_in - m_new[..., None])
        decay = jnp.exp(total + m_state - m_new)
        c_state = decay[..., None, None] * c_state + jnp.einsum('bhj,bhjd,bhjv->bhdv', s_in, kc, vc)
        n_state = decay[..., None] * n_state + jnp.einsum('bhj,bhjd->bhd', s_in, kc)
        return (c_state, n_state, m_new), out

    init = (jnp.zeros((b, h, dk, dv), jnp.float32), jnp.zeros((b, h, dk), jnp.float32),
            jnp.zeros((b, h), jnp.float32))
    xs = (to_chunks(q), to_chunks(k), to_chunks(v), to_chunks(i_pre), to_chunks(log_f))
    _, out = lax.scan(step, init, xs)
    return from_chunks(out)


def retention_scan(q, k, v, log_gamma, include_diag):
    b, h, _, dk = q.shape
    dv = v.shape[-1]
    idx = jnp.arange(CHUNK, dtype=jnp.float32)
    diff = idx[:, None] - idx[None, :]
    mask = diff >= 0 if include_diag else diff > 0
    intra_decay = jnp.where(mask, jnp.exp(jnp.where(mask, diff, 0.0)[None] * log_gamma[:, None, None]), 0.0)
    q_decay = jnp.exp((idx + 1.0)[None, :] * log_gamma[:, None])
    k_decay = jnp.exp((CHUNK - 1.0 - idx)[None, :] * log_gamma[:, None])
    chunk_decay = jnp.exp(CHUNK * log_gamma)

    def step(s_state, inp):
        qc, kc, vc = inp
        scores = jnp.einsum('bhid,bhjd->bhij', qc, kc) * intra_decay[None]
        out = (jnp.einsum('bhij,bhjv->bhiv', scores, vc)
               + q_decay[None, :, :, None] * jnp.einsum('bhid,bhdv->bhiv', qc, s_state))
        s_state = (chunk_decay[None, :, None, None] * s_state
                   + jnp.einsum('bhjd,bhjv->bhdv', kc * k_decay[None, :, :, None], vc))
        return s_state, out

    init = jnp.zeros((b, h, dk, dv), jnp.float32)
    _, out = lax.scan(step, init, (to_chunks(q), to_chunks(k), to_chunks(v)))
    return from_chunks(out)


def hybrid_layer(x, cos, sin, log_gamma, norm_w, w_in, b_mgate, conv_w, conv_b,
                 m_norm_w, r_norm_w, w_proj_m, w_proj_r, b_mix, w_out):
    b, s, _ = x.shape
    f32 = jnp.float32
    h = rmsnorm(x, norm_w)
    proj = jnp.einsum('bsd,de->bse', h, w_in)
    (q_m, k_m, v_m, z_m, o_m, g_m, q_r, k_r, v_r, z_r, gate_m, gate_r) = jnp.split(
        proj, np.cumsum(IN_SPLITS)[:-1].tolist(), axis=-1)

    qk_m = jax.nn.silu(depthwise_conv(jnp.concatenate([q_m, k_m], axis=-1), conv_w, conv_b))
    q_m = split_heads(qk_m[..., :WM_QK], NH_M).astype(f32) * DQK_M ** -0.5
    k_m = split_heads(qk_m[..., WM_QK:], NH_M).astype(f32)
    v_m = split_heads(v_m, NH_M).astype(f32)
    gates = (g_m + b_mgate).astype(f32).reshape(b, s, 4, NH_M).transpose(2, 0, 3, 1)
    h_m = (mlstm_scan(q_m, k_m, v_m, gates[0], gates[1])
           + flip_seq(mlstm_scan(flip_seq(q_m), flip_seq(k_m), flip_seq(v_m),
                                 flip_seq(gates[2]), flip_seq(gates[3]))))
    h_m = jax.nn.sigmoid(split_heads(o_m, NH_M).astype(f32)) * h_m
    u_m = head_rmsnorm(h_m, m_norm_w).astype(x.dtype) * jax.nn.silu(z_m)

    q_r = rope(q_r.reshape(b, s, NH_R, DQK_R).astype(f32), cos, sin).transpose(0, 2, 1, 3) * DQK_R ** -0.5
    k_r = rope(k_r.reshape(b, s, NH_R, DQK_R).astype(f32), cos, sin).transpose(0, 2, 1, 3)
    v_r = split_heads(v_r, NH_R).astype(f32)
    h_r = (retention_scan(q_r, k_r, v_r, log_gamma, True)
           + flip_seq(retention_scan(flip_seq(q_r), flip_seq(k_r), flip_seq(v_r), log_gamma, False)))
    u_r = head_rmsnorm(h_r, r_norm_w).astype(x.dtype) * jax.nn.silu(z_r)

    mix = jax.nn.sigmoid(jnp.concatenate([gate_m, gate_r], axis=-1) + b_mix)
    y = (mix[..., :D_MODEL] * jnp.einsum('bse,ed->bsd', u_m, w_proj_m)
         + mix[..., D_MODEL:] * jnp.einsum('bse,ed->bsd', u_r, w_proj_r))
    return x + jnp.einsum('bsd,de->bse', y, w_out)


def setup_inputs(seed: int = 0) -> dict:
    key = jax.random.key(seed)
    ks = jax.random.split(key, 16)
    nrm = jax.random.normal
    x = nrm(ks[0], (BATCH, SEQ, D_MODEL), jnp.float32)
    offset = jax.random.randint(ks[1], (BATCH, 1), 0, 4096, dtype=jnp.int32)
    positions = offset + jnp.arange(SEQ, dtype=jnp.int32)[None, :]
    norm_w = 1.0 + 0.02 * nrm(ks[2], (DEPTH, D_MODEL), jnp.float32)
    w_in = nrm(ks[3], (DEPTH, D_MODEL, D_IN), jnp.float32) * D_MODEL ** -0.5
    i_bias = 0.1 * nrm(ks[4], (DEPTH, 2, NH_M), jnp.float32)
    f_bias = jnp.linspace(3.0, 6.0, NH_M, dtype=jnp.float32) + 0.1 * nrm(ks[5], (DEPTH, 2, NH_M), jnp.float32)
    b_mgate = jnp.stack([i_bias[:, 0], f_bias[:, 0], i_bias[:, 1], f_bias[:, 1]], axis=1).reshape(DEPTH, N_MGATE)
    conv_w = nrm(ks[6], (DEPTH, CONV_W, 2 * WM_QK), jnp.float32) * CONV_W ** -0.5
    conv_b = 0.02 * nrm(ks[7], (DEPTH, 2 * WM_QK), jnp.float32)
    m_norm_w = 1.0 + 0.02 * nrm(ks[8], (DEPTH, WM_V), jnp.float32)
    r_norm_w = 1.0 + 0.02 * nrm(ks[9], (DEPTH, WR_V), jnp.float32)
    w_proj_m = nrm(ks[10], (DEPTH, WM_V, D_MODEL), jnp.float32) * WM_V ** -0.5
    w_proj_r = nrm(ks[11], (DEPTH, WR_V, D_MODEL), jnp.float32) * WR_V ** -0.5
    b_mix = 0.02 * nrm(ks[12], (DEPTH, 2 * D_MODEL), jnp.float32)
    w_out = nrm(ks[13], (DEPTH, D_MODEL, D_MODEL), jnp.float32) * D_MODEL ** -0.5
    final_norm_w = 1.0 + 0.02 * nrm(ks[14], (D_MODEL,), jnp.float32)
    return {'x': x, 'positions': positions, 'norm_w': norm_w, 'w_in': w_in, 'b_mgate': b_mgate,
            'conv_w': conv_w, 'conv_b': conv_b, 'm_norm_w': m_norm_w, 'r_norm_w': r_norm_w,
            'w_proj_m': w_proj_m, 'w_proj_r': w_proj_r, 'b_mix': b_mix, 'w_out': w_out,
            'final_norm_w': final_norm_w}


def reference(x, positions, norm_w, w_in, b_mgate, conv_w, conv_b, m_norm_w, r_norm_w,
              w_proj_m, w_proj_r, b_mix, w_out, final_norm_w):
    log_gamma = jnp.log1p(-jnp.power(2.0, -5.0 - jnp.arange(NH_R, dtype=jnp.float32)))
    inv_freq = jnp.power(ROPE_BASE, -jnp.arange(DQK_R // 2, dtype=jnp.float32) / (DQK_R // 2))
    angle = positions.astype(jnp.float32)[..., None] * inv_freq
    cos = jnp.cos(angle)[:, :, None, :]
    sin = jnp.sin(angle)[:, :, None, :]
    for layer in range(DEPTH):
        x = hybrid_layer(x, cos, sin, log_gamma, norm_w[layer], w_in[layer], b_mgate[layer],
                         conv_w[layer], conv_b[layer], m_norm_w[layer], r_norm_w[layer],
                         w_proj_m[layer], w_proj_r[layer], b_mix[layer], w_out[layer])
    return rmsnorm(x, final_norm_w)
```

```python
import functools

import jax
import jax.numpy as jnp
from jax import lax
from jax.experimental import pallas as pl
from jax.experimental.pallas import tpu as pltpu

NH_M = 4
NH_R = 8
N_GATE_TYPES = 4
CONV_W = 5
ROPE_BASE = 10000.0
EPS = 1e-6

LANE = 128
SUBLANE = 8
MIB = 1024 * 1024

CHUNK_M = 256
CHUNK_R = 256

F32 = jnp.float32
BF16 = jnp.bfloat16

_NT = (((1,), (1,)), ((), ()))
_TN = (((0,), (0,)), ((), ()))


def _params(semantics, vmem_mib):
    return pltpu.CompilerParams(dimension_semantics=semantics, vmem_limit_bytes=vmem_mib * MIB)


def _sigmoid(x):
    return 1.0 / (1.0 + jnp.exp(-x))


def _inproj_kernel(x_ref, nw_ref, w_ref, wg_ref, bg_ref, o_ref, g_ref, h_scr, *, sub_rows):
    tm = x_ref.shape[0]

    @pl.when(pl.program_id(1) == 0)
    def _():
        def body(r, carry):
            rs = pl.ds(pl.multiple_of(r * sub_rows, sub_rows), sub_rows)
            x = x_ref[rs, :]
            ms = jnp.mean(x * x, axis=-1, keepdims=True)
            h = (x * lax.rsqrt(ms + EPS)) * nw_ref[...]
            h_scr[rs, :] = h.astype(BF16)
            return carry

        lax.fori_loop(0, tm // sub_rows, body, 0)
        g_ref[...] = jnp.dot(h_scr[...], wg_ref[...], preferred_element_type=F32) + bg_ref[...]

    o_ref[...] = jnp.dot(h_scr[...], w_ref[...], preferred_element_type=F32)


def _inproj(x2, norm_w, w_main, w_gate, b_gate, *, tm=1024, tn=1024):
    s, d = x2.shape
    n = w_main.shape[1]
    return pl.pallas_call(
        functools.partial(_inproj_kernel, sub_rows=256),
        grid=(s // tm, n // tn),
        in_specs=[
            pl.BlockSpec((tm, d), lambda i, j: (i, 0)),
            pl.BlockSpec((1, d), lambda i, j: (0, 0)),
            pl.BlockSpec((d, tn), lambda i, j: (0, j)),
            pl.BlockSpec((d, LANE), lambda i, j: (0, 0)),
            pl.BlockSpec((1, LANE), lambda i, j: (0, 0)),
        ],
        out_specs=[
            pl.BlockSpec((tm, tn), lambda i, j: (i, j)),
            pl.BlockSpec((tm, LANE), lambda i, j: (i, 0)),
        ],
        out_shape=[jax.ShapeDtypeStruct((s, n), F32), jax.ShapeDtypeStruct((s, LANE), F32)],
        scratch_shapes=[pltpu.VMEM((tm, d), BF16)],
        compiler_params=_params(("parallel", "arbitrary"), 48),
        name="inproj",
    )(x2, norm_w, w_main, w_gate, b_gate)


def _gate_prep_kernel(g_ref, p_ref):
    g = g_ref[...]
    chunk = g.shape[0]
    log_f = jnp.minimum(g, 0.0) - jnp.log1p(jnp.exp(-jnp.abs(g)))
    ri = lax.broadcasted_iota(jnp.int32, (chunk, chunk), 0)
    ci = lax.broadcasted_iota(jnp.int32, (chunk, chunk), 1)
    tril = (ci <= ri).astype(F32)
    triu = (ci >= ri).astype(F32)
    cum_prefix = jnp.dot(tril, log_f, precision=lax.Precision.HIGHEST, preferred_element_type=F32)
    cum_suffix = jnp.dot(triu, log_f, precision=lax.Precision.HIGHEST, preferred_element_type=F32)
    lane = lax.broadcasted_iota(jnp.int32, g.shape, 1)
    cum = jnp.where(lane < 2 * NH_M, cum_prefix, cum_suffix)
    b = g - pltpu.roll(cum, LANE - NH_M, axis=1)
    p_ref[...] = jnp.where((lane % (2 * NH_M)) < NH_M, b, cum)


def _gate_prep(g, chunk):
    s = g.shape[0]
    return pl.pallas_call(
        _gate_prep_kernel,
        grid=(s // chunk,),
        in_specs=[pl.BlockSpec((chunk, LANE), lambda c: (c, 0))],
        out_specs=pl.BlockSpec((chunk, LANE), lambda c: (c, 0)),
        out_shape=jax.ShapeDtypeStruct((s, LANE), F32),
        compiler_params=_params(("parallel",), 32),
        name="gate_prep",
    )(g)


def _conv_kernel(xp_ref, x_ref, xn_ref, w_ref, b_ref, o_ref, xe_ref, *, q_col_blocks, q_scale):
    i = pl.program_id(0)
    tb = x_ref.shape[0]
    halo = CONV_W // 2
    xe_ref[0:SUBLANE, :] = jnp.where(i > 0, xp_ref[...], 0.0)
    xe_ref[SUBLANE:SUBLANE + tb, :] = x_ref[...]
    xe_ref[SUBLANE + tb:2 * SUBLANE + tb, :] = jnp.where(i < pl.num_programs(0) - 1, xn_ref[...], 0.0)
    acc = b_ref[...] + w_ref[0:1, :] * xe_ref[pl.ds(SUBLANE - halo, tb), :]
    for k in range(1, CONV_W):
        acc = acc + w_ref[k:k + 1, :] * xe_ref[pl.ds(SUBLANE - halo + k, tb), :]
    y = acc * _sigmoid(acc)
    scale = jnp.where(pl.program_id(1) < q_col_blocks, q_scale, 1.0).astype(F32)
    o_ref[...] = (y * scale).astype(BF16)


def _conv_silu(proj, conv_w, conv_b, wm_qk, q_scale, *, tb=512, tc=512):
    s = proj.shape[0]
    width = 2 * wm_qk
    rb = tb // SUBLANE
    n_rb = s // SUBLANE
    return pl.pallas_call(
        functools.partial(_conv_kernel, q_col_blocks=wm_qk // tc, q_scale=q_scale),
        grid=(s // tb, width // tc),
        in_specs=[
            pl.BlockSpec((SUBLANE, tc), lambda i, j: (jnp.maximum(i * rb - 1, 0), j)),
            pl.BlockSpec((tb, tc), lambda i, j: (i, j)),
            pl.BlockSpec((SUBLANE, tc), lambda i, j: (jnp.minimum((i + 1) * rb, n_rb - 1), j)),
            pl.BlockSpec((CONV_W, tc), lambda i, j: (0, j)),
            pl.BlockSpec((1, tc), lambda i, j: (0, j)),
        ],
        out_specs=pl.BlockSpec((tb, tc), lambda i, j: (i, j)),
        out_shape=jax.ShapeDtypeStruct((s, width), BF16),
        scratch_shapes=[pltpu.VMEM((tb + 2 * SUBLANE, tc), F32)],
        compiler_params=_params(("parallel", "parallel"), 32),
        name="conv_silu",
    )(proj, proj, proj, conv_w, conv_b)


def _mlstm_direction(q_ref, k_ref, v_ref, pc_ref, pr_ref, o_ref, c_ref, n_ref, m_ref, *, reverse):
    chunk = q_ref.shape[0]
    q = q_ref[...]
    k = k_ref[...]
    v = v_ref[...].astype(BF16)
    pc = pc_ref[...]
    pr = pr_ref[...]
    t0 = 2 if reverse else 0
    b_col = pc[:, t0:t0 + 1]
    cum_col = pc[:, t0 + 1:t0 + 2]
    b_row = pr[t0:t0 + 1, :]
    total = cum_col[0:1, :] if reverse else cum_col[chunk - 1:chunk, :]
    m_prev = m_ref[...]

    ri = lax.broadcasted_iota(jnp.int32, (chunk, chunk), 0)
    ci = lax.broadcasted_iota(jnp.int32, (chunk, chunk), 1)
    mask = (ci >= ri) if reverse else (ci <= ri)
    d_log = jnp.where(mask, cum_col + b_row, -jnp.inf)
    inter_log = cum_col + m_prev
    m_row = jnp.maximum(jnp.max(d_log, axis=1, keepdims=True), inter_log)
    scores = lax.dot_general(q, k, _NT, preferred_element_type=F32)
    w = jnp.exp(d_log - m_row) * scores
    a = jnp.exp(inter_log - m_row)
    q_c = jnp.dot(q, c_ref[...].astype(BF16), preferred_element_type=F32)
    num = jnp.dot(w.astype(BF16), v, preferred_element_type=F32) + a * q_c
    q_n = jnp.sum(q.astype(F32) * n_ref[...], axis=1, keepdims=True)
    den = jnp.sum(w, axis=1, keepdims=True) + a * q_n
    o_ref[...] = num / jnp.maximum(jnp.abs(den), jnp.exp(-m_row))

    m_new = jnp.maximum(total + m_prev, jnp.max(total + b_row, axis=1, keepdims=True))
    s_in = jnp.exp(total + b_col - m_new)
    decay = jnp.exp(total + m_prev - m_new)
    k_in = k.astype(F32) * s_in
    c_ref[...] = decay * c_ref[...] + lax.dot_general(k_in.astype(BF16), v, _TN, preferred_element_type=F32)
    n_ref[...] = decay * n_ref[...] + jnp.sum(k_in, axis=0, keepdims=True)
    m_ref[...] = m_new


def _mlstm_kernel(qf, kf, vf, pcf, prf, qb, kb, vb, pcb, prb, of, ob, cf, nf, mf, cb, nb, mb):
    @pl.when(pl.program_id(1) == 0)
    def _():
        for ref in (cf, nf, mf, cb, nb, mb):
            ref[...] = jnp.zeros(ref.shape, ref.dtype)

    _mlstm_direction(qf, kf, vf, pcf, prf, of, cf, nf, mf, reverse=False)
    _mlstm_direction(qb, kb, vb, pcb, prb, ob, cb, nb, mb, reverse=True)


def _mlstm_scan(qk, proj, p_col, p_row, *, dk, dv, v_col0, chunk):
    s = qk.shape[0]
    nc = s // chunk
    k_blk0 = (NH_M * dk) // dk
    v_blk0 = v_col0 // dv

    def fwd(c):
        return c

    def bwd(c):
        return nc - 1 - c

    in_specs = []
    for cmap in (fwd, bwd):
        in_specs += [
            pl.BlockSpec((chunk, dk), lambda h, c, cmap=cmap: (cmap(c), h)),
            pl.BlockSpec((chunk, dk), lambda h, c, cmap=cmap: (cmap(c), k_blk0 + h)),
            pl.BlockSpec((chunk, dv), lambda h, c, cmap=cmap: (cmap(c), v_blk0 + h)),
            pl.BlockSpec((None, chunk, N_GATE_TYPES), lambda h, c, cmap=cmap: (h, cmap(c), 0)),
            pl.BlockSpec((None, N_GATE_TYPES, chunk), lambda h, c, cmap=cmap: (h, 0, cmap(c))),
        ]
    out_specs = [
        pl.BlockSpec((chunk, dv), lambda h, c: (c, h)),
        pl.BlockSpec((chunk, dv), lambda h, c: (nc - 1 - c, h)),
    ]
    state = [pltpu.VMEM((dk, dv), F32), pltpu.VMEM((1, dk), F32), pltpu.VMEM((1, 1), F32)]
    return pl.pallas_call(
        _mlstm_kernel,
        grid=(NH_M, nc),
        in_specs=in_specs,
        out_specs=out_specs,
        out_shape=[jax.ShapeDtypeStruct((s, NH_M * dv), F32)] * 2,
        scratch_shapes=state + state,
        compiler_params=_params(("parallel", "arbitrary"), 32),
        name="mlstm_scan",
    )(qk, qk, proj, p_col, p_row, qk, qk, proj, p_col, p_row)


def _rope_table_kernel(pos_ref, freq_ref, sign_ref, cos_ref, sin_ref):
    angle = pos_ref[...].astype(F32) * freq_ref[...]
    cos_ref[...] = jnp.cos(angle)
    sin_ref[...] = jnp.sin(angle) * sign_ref[...]


def _rope_tables(pos_col, freq_full, sign, *, tb=1024):
    s = pos_col.shape[0]
    dk = freq_full.shape[1]
    return pl.pallas_call(
        _rope_table_kernel,
        grid=(s // tb,),
        in_specs=[
            pl.BlockSpec((tb, 1), lambda i: (i, 0)),
            pl.BlockSpec((1, dk), lambda i: (0, 0)),
            pl.BlockSpec((1, dk), lambda i: (0, 0)),
        ],
        out_specs=[pl.BlockSpec((tb, dk), lambda i: (i, 0))] * 2,
        out_shape=[jax.ShapeDtypeStruct((s, dk), F32)] * 2,
        compiler_params=_params(("parallel",), 32),
        name="rope_tables",
    )(pos_col, freq_full, sign)


def _retention_direction(q_ref, k_ref, v_ref, cos_ref, sin_ref, o_ref, s_ref, d_ref, qd_ref, kd_ref, cd, *, q_scale):
    dk = q_ref.shape[1]
    cos = cos_ref[...]
    sin = sin_ref[...]
    q = q_ref[...]
    k = k_ref[...]
    q = (q * cos + pltpu.roll(q, dk // 2, axis=1) * sin) * q_scale
    k = k * cos + pltpu.roll(k, dk // 2, axis=1) * sin
    qb = q.astype(BF16)
    v = v_ref[...].astype(BF16)
    scores = lax.dot_general(qb, k.astype(BF16), _NT, preferred_element_type=F32) * d_ref[...]
    inter = jnp.dot(qb, s_ref[...].astype(BF16), preferred_element_type=F32)
    o_ref[...] = jnp.dot(scores.astype(BF16), v, preferred_element_type=F32) + qd_ref[...] * inter
    k_in = (k * kd_ref[...]).astype(BF16)
    s_ref[...] = cd * s_ref[...] + lax.dot_general(k_in, v, _TN, preferred_element_type=F32)


def _retention_kernel(lg_ref, qf, kf, vf, cosf, sinf, qb, kb, vb, cosb, sinb, of, ob,
                      sf, sb, df, db, qdf, kdf, qdb, kdb, *, q_scale):
    chunk = qf.shape[0]
    lg = lg_ref[...]

    @pl.when(pl.program_id(1) == 0)
    def _():
        sf[...] = jnp.zeros(sf.shape, F32)
        sb[...] = jnp.zeros(sb.shape, F32)
        ri = lax.broadcasted_iota(jnp.int32, (chunk, chunk), 0)
        ci = lax.broadcasted_iota(jnp.int32, (chunk, chunk), 1)
        diff = (ri - ci).astype(F32)
        df[...] = jnp.where(diff >= 0.0, jnp.exp(jnp.where(diff >= 0.0, diff, 0.0) * lg), 0.0)
        db[...] = jnp.where(diff < 0.0, jnp.exp(jnp.where(diff < 0.0, -diff, 0.0) * lg), 0.0)
        idx = lax.broadcasted_iota(jnp.int32, (chunk, 1), 0).astype(F32)
        qdf[...] = jnp.exp((idx + 1.0) * lg)
        kdf[...] = jnp.exp((chunk - 1.0 - idx) * lg)
        qdb[...] = jnp.exp((chunk - idx) * lg)
        kdb[...] = jnp.exp(idx * lg)

    cd = jnp.exp(chunk * lg)
    _retention_direction(qf, kf, vf, cosf, sinf, of, sf, df, qdf, kdf, cd, q_scale=q_scale)
    _retention_direction(qb, kb, vb, cosb, sinb, ob, sb, db, qdb, kdb, cd, q_scale=q_scale)


def _retention_scan(proj, cos_full, sin_signed, log_gamma, *, dk, dv, q_col0, k_col0, v_col0, chunk):
    s = proj.shape[0]
    nc = s // chunk
    q_blk0, k_blk0, v_blk0 = q_col0 // dk, k_col0 // dk, v_col0 // dv

    def fwd(c):
        return c

    def bwd(c):
        return nc - 1 - c

    in_specs = [pl.BlockSpec((None, 1, 1), lambda h, c: (h, 0, 0))]
    for cmap in (fwd, bwd):
        in_specs += [
            pl.BlockSpec((chunk, dk), lambda h, c, cmap=cmap: (cmap(c), q_blk0 + h)),
            pl.BlockSpec((chunk, dk), lambda h, c, cmap=cmap: (cmap(c), k_blk0 + h)),
            pl.BlockSpec((chunk, dv), lambda h, c, cmap=cmap: (cmap(c), v_blk0 + h)),
            pl.BlockSpec((chunk, dk), lambda h, c, cmap=cmap: (cmap(c), 0)),
            pl.BlockSpec((chunk, dk), lambda h, c, cmap=cmap: (cmap(c), 0)),
        ]
    out_specs = [
        pl.BlockSpec((chunk, dv), lambda h, c: (c, h)),
        pl.BlockSpec((chunk, dv), lambda h, c: (nc - 1 - c, h)),
    ]
    scratch = [
        pltpu.VMEM((dk, dv), F32), pltpu.VMEM((dk, dv), F32),
        pltpu.VMEM((chunk, chunk), F32), pltpu.VMEM((chunk, chunk), F32),
        pltpu.VMEM((chunk, 1), F32), pltpu.VMEM((chunk, 1), F32),
        pltpu.VMEM((chunk, 1), F32), pltpu.VMEM((chunk, 1), F32),
    ]
    return pl.pallas_call(
        functools.partial(_retention_kernel, q_scale=dk ** -0.5),
        grid=(NH_R, nc),
        in_specs=in_specs,
        out_specs=out_specs,
        out_shape=[jax.ShapeDtypeStruct((s, NH_R * dv), F32)] * 2,
        scratch_shapes=scratch,
        compiler_params=_params(("parallel", "arbitrary"), 32),
        name="retention_scan",
    )(log_gamma, proj, proj, proj, cos_full, sin_signed, proj, proj, proj, cos_full, sin_signed)


def _head_norm_gate(h, gain_ref, z, u_ref, n_heads):
    width = h.shape[1] // n_heads
    for hh in range(n_heads):
        cols = slice(hh * width, (hh + 1) * width)
        seg = h[:, cols]
        y = seg * lax.rsqrt(jnp.mean(seg * seg, axis=-1, keepdims=True) + EPS) * gain_ref[:, cols]
        zz = z[:, cols]
        u_ref[:, cols] = (y * (zz * _sigmoid(zz))).astype(BF16)


def _merge_kernel(hmf, hmb, om, zm, hrf, hrb, zr, mw, rw, wpm, wpr, gm, gr, bm, br, y_ref, um, ur):
    @pl.when(pl.program_id(1) == 0)
    def _():
        h_m = _sigmoid(om[...]) * (hmf[...] + hmb[...])
        _head_norm_gate(h_m, mw, zm[...], um, NH_M)
        _head_norm_gate(hrf[...] + hrb[...], rw, zr[...], ur, NH_R)

    y_m = jnp.dot(um[...], wpm[...], preferred_element_type=F32)
    y_r = jnp.dot(ur[...], wpr[...], preferred_element_type=F32)
    y = _sigmoid(gm[...] + bm[...]) * y_m + _sigmoid(gr[...] + br[...]) * y_r
    y_ref[...] = y.astype(BF16)


def _merge(hmf, hmb, hrf, hrb, proj, m_norm_w, r_norm_w, w_proj_m, w_proj_r, b_mix, *,
           om_col0, zm_col0, zr_col0, gm_col0, gr_col0, tm=256, tn=512):
    s, d = hmf.shape
    nj = d // tn
    wide = lambda col0: pl.BlockSpec((tm, d), lambda i, j: (i, col0 // d))
    row_full = pl.BlockSpec((tm, d), lambda i, j: (i, 0))
    return pl.pallas_call(
        _merge_kernel,
        grid=(s // tm, nj),
        in_specs=[
            row_full, row_full, wide(om_col0), wide(zm_col0),
            row_full, row_full, wide(zr_col0),
            pl.BlockSpec((1, d), lambda i, j: (0, 0)),
            pl.BlockSpec((1, d), lambda i, j: (0, 0)),
            pl.BlockSpec((d, tn), lambda i, j: (0, j)),
            pl.BlockSpec((d, tn), lambda i, j: (0, j)),
            pl.BlockSpec((tm, tn), lambda i, j: (i, gm_col0 // tn + j)),
            pl.BlockSpec((tm, tn), lambda i, j: (i, gr_col0 // tn + j)),
            pl.BlockSpec((1, tn), lambda i, j: (0, j)),
            pl.BlockSpec((1, tn), lambda i, j: (0, nj + j)),
        ],
        out_specs=pl.BlockSpec((tm, tn), lambda i, j: (i, j)),
        out_shape=jax.ShapeDtypeStruct((s, d), BF16),
        scratch_shapes=[pltpu.VMEM((tm, d), BF16), pltpu.VMEM((tm, d), BF16)],
        compiler_params=_params(("parallel", "arbitrary"), 56),
        name="merge",
    )(hmf, hmb, proj, proj, hrf, hrb, proj, m_norm_w, r_norm_w, w_proj_m, w_proj_r, proj, proj, b_mix, b_mix)


def _outproj_kernel(x_ref, y_ref, w_ref, fw_ref, o_ref, *, final_norm):
    out = x_ref[...] + jnp.dot(y_ref[...], w_ref[...], preferred_element_type=F32)
    if final_norm:
        ms = jnp.mean(out * out, axis=-1, keepdims=True)
        out = (out * lax.rsqrt(ms + EPS)) * fw_ref[...]
    o_ref[...] = out


def _outproj(x2, y, w_out, final_w, *, final_norm, tm=512):
    s, d = x2.shape
    return pl.pallas_call(
        functools.partial(_outproj_kernel, final_norm=final_norm),
        grid=(s // tm,),
        in_specs=[
            pl.BlockSpec((tm, d), lambda i: (i, 0)),
            pl.BlockSpec((tm, d), lambda i: (i, 0)),
            pl.BlockSpec((d, d), lambda i: (0, 0)),
            pl.BlockSpec((1, d), lambda i: (0, 0)),
        ],
        out_specs=pl.BlockSpec((tm, d), lambda i: (i, 0)),
        out_shape=jax.ShapeDtypeStruct((s, d), F32),
        compiler_params=_params(("parallel",), 48),
        name="outproj",
    )(x2, y, w_out, final_w)


def kernel(x, positions, norm_w, w_in, b_mgate, conv_w, conv_b, m_norm_w, r_norm_w,
           w_proj_m, w_proj_r, b_mix, w_out, final_norm_w):
    batch, s, d = x.shape
    assert batch == 1, "kernel is written for BATCH == 1"
    depth = norm_w.shape[0]
    dqk_m, dv_m = d // 8, d // 4
    dqk_r, dv_r = d // 16, d // 8
    wm_qk, wm_v = NH_M * dqk_m, NH_M * dv_m
    wr_qk, wr_v = NH_R * dqk_r, NH_R * dv_r
    n_gate = N_GATE_TYPES * NH_M
    col_v_m = 2 * wm_qk
    col_z_m = col_v_m + wm_v
    col_o_m = col_z_m + wm_v
    col_gate = col_o_m + wm_v
    col_q_r = col_gate
    col_k_r = col_q_r + wr_qk
    col_v_r = col_k_r + wr_qk
    col_z_r = col_v_r + wr_v
    col_gate_m = col_z_r + wr_v
    col_gate_r = col_gate_m + d

    x2 = x.reshape(s, d)
    log_gamma = jnp.log1p(-jnp.power(2.0, -5.0 - jnp.arange(NH_R, dtype=F32))).reshape(NH_R, 1, 1)
    inv_freq = jnp.power(ROPE_BASE, -jnp.arange(dqk_r // 2, dtype=F32) / (dqk_r // 2))
    freq_full = jnp.concatenate([inv_freq, inv_freq]).reshape(1, dqk_r)
    sign = jnp.concatenate([-jnp.ones((dqk_r // 2,), F32), jnp.ones((dqk_r // 2,), F32)]).reshape(1, dqk_r)
    cos_full, sin_signed = _rope_tables(positions.reshape(s, 1), freq_full, sign)

    for layer in range(depth):
        w_l = w_in[layer]
        w_main = jnp.concatenate([w_l[:, :col_gate], w_l[:, col_gate + n_gate:]], axis=1).astype(BF16)
        w_gate = jnp.pad(w_l[:, col_gate:col_gate + n_gate], ((0, 0), (0, LANE - n_gate))).astype(BF16)
        b_gate = jnp.pad(b_mgate[layer], (0, LANE - n_gate)).reshape(1, LANE)

        proj, gates = _inproj(x2, norm_w[layer].reshape(1, d), w_main, w_gate, b_gate)

        p = _gate_prep(gates, CHUNK_M)[:, :n_gate].reshape(s, N_GATE_TYPES, NH_M)
        p_col = p.transpose(2, 0, 1)
        p_row = p.transpose(2, 1, 0)

        qk = _conv_silu(proj, conv_w[layer], conv_b[layer].reshape(1, 2 * wm_qk), wm_qk, dqk_m ** -0.5)
        hmf, hmb = _mlstm_scan(qk, proj, p_col, p_row, dk=dqk_m, dv=dv_m, v_col0=col_v_m, chunk=CHUNK_M)
        hrf, hrb = _retention_scan(proj, cos_full, sin_signed, log_gamma, dk=dqk_r, dv=dv_r,
                                   q_col0=col_q_r, k_col0=col_k_r, v_col0=col_v_r, chunk=CHUNK_R)
        y = _merge(hmf, hmb, hrf, hrb, proj, m_norm_w[layer].reshape(1, wm_v), r_norm_w[layer].reshape(1, wr_v),
                   w_proj_m[layer].astype(BF16), w_proj_r[layer].astype(BF16), b_mix[layer].reshape(1, 2 * d),
                   om_col0=col_o_m, zm_col0=col_z_m, zr_col0=col_z_r, gm_col0=col_gate_m, gr_col0=col_gate_r)
        x2 = _outproj(x2, y, w_out[layer].astype(BF16), final_norm_w.reshape(1, d),
                      final_norm=(layer == depth - 1))
    return x2.reshape(batch, s, d)
```

```python
import functools

import jax
import jax.numpy as jnp
from jax import lax
from jax.experimental import pallas as pl
from jax.experimental.pallas import tpu as pltpu

NH_M = 4
NH_R = 8
N_GATE_TYPES = 4
CONV_W = 5
ROPE_BASE = 10000.0
EPS = 1e-6

LANE = 128
BF16_ROWS = 16
MIB = 1024 * 1024

CHUNK_M = 256
CHUNK_R = 256

F32 = jnp.float32
BF16 = jnp.bfloat16


def _params(semantics, vmem_mib):
    return pltpu.CompilerParams(dimension_semantics=semantics, vmem_limit_bytes=vmem_mib * MIB)


def _sigmoid(x):
    return 1.0 / (1.0 + jnp.exp(-x))


def _lanes(x, width):
    return x if width == LANE else jnp.concatenate([x] * (width // LANE), axis=1)


def _dot(a, b):
    return jnp.dot(a, b, preferred_element_type=F32)


def _inproj_kernel(x_ref, nw_ref, w_ref, wg_ref, bg_ref, o_ref, g_ref, h_scr, *, sub_rows):
    tm = x_ref.shape[0]

    @pl.when(pl.program_id(1) == 0)
    def _():
        def body(r, carry):
            rs = pl.ds(pl.multiple_of(r * sub_rows, sub_rows), sub_rows)
            x = x_ref[rs, :]
            ms = jnp.mean(x * x, axis=-1, keepdims=True)
            h = (x * lax.rsqrt(ms + EPS)) * nw_ref[...]
            h_scr[rs, :] = h.astype(BF16)
            return carry

        lax.fori_loop(0, tm // sub_rows, body, 0)
        g_ref[...] = _dot(h_scr[...], wg_ref[...]) + bg_ref[...]

    o_ref[...] = _dot(h_scr[...], w_ref[...]).astype(o_ref.dtype)


def _inproj(x2, norm_w, w_main, w_gate, b_gate, *, tm=1024, tn=1024):
    s, d = x2.shape
    n = w_main.shape[1]
    return pl.pallas_call(
        functools.partial(_inproj_kernel, sub_rows=256),
        grid=(s // tm, n // tn),
        in_specs=[
            pl.BlockSpec((tm, d), lambda i, j: (i, 0)),
            pl.BlockSpec((1, d), lambda i, j: (0, 0)),
            pl.BlockSpec((d, tn), lambda i, j: (0, j)),
            pl.BlockSpec((d, LANE), lambda i, j: (0, 0)),
            pl.BlockSpec((1, LANE), lambda i, j: (0, 0)),
        ],
        out_specs=[
            pl.BlockSpec((tm, tn), lambda i, j: (i, j)),
            pl.BlockSpec((tm, LANE), lambda i, j: (i, 0)),
        ],
        out_shape=[jax.ShapeDtypeStruct((s, n), BF16), jax.ShapeDtypeStruct((s, LANE), F32)],
        scratch_shapes=[pltpu.VMEM((tm, d), BF16)],
        compiler_params=_params(("parallel", "arbitrary"), 48),
        name="inproj",
    )(x2, norm_w, w_main, w_gate, b_gate)


def _gate_prep_kernel(g_ref, p_ref, pm_ref):
    g = g_ref[...]
    chunk = g.shape[0]
    log_f = jnp.minimum(g, 0.0) - jnp.log1p(jnp.exp(-jnp.abs(g)))
    ri = lax.broadcasted_iota(jnp.int32, (chunk, chunk), 0)
    ci = lax.broadcasted_iota(jnp.int32, (chunk, chunk), 1)
    tril = (ci <= ri).astype(F32)
    triu = (ci >= ri).astype(F32)
    cum_prefix = jnp.dot(tril, log_f, precision=lax.Precision.HIGHEST, preferred_element_type=F32)
    cum_suffix = jnp.dot(triu, log_f, precision=lax.Precision.HIGHEST, preferred_element_type=F32)
    lane = lax.broadcasted_iota(jnp.int32, g.shape, 1)
    row = lax.broadcasted_iota(jnp.int32, g.shape, 0)
    cum = jnp.where(lane < 2 * NH_M, cum_prefix, cum_suffix)
    b = g - pltpu.roll(cum, LANE - NH_M, axis=1)
    p_ref[...] = jnp.where((lane % (2 * NH_M)) < NH_M, b, cum)

    pm_f = b
    pm_b = b
    step = 1
    while step < chunk:
        pm_f = jnp.maximum(pm_f, jnp.where(row >= step, pltpu.roll(pm_f, step, axis=0), -jnp.inf))
        pm_b = jnp.maximum(pm_b, jnp.where(row < chunk - step, pltpu.roll(pm_b, chunk - step, axis=0), -jnp.inf))
        step *= 2
    pm_ref[...] = jnp.where(lane < 2 * NH_M, pm_f, pm_b)


def _gate_prep(g, chunk):
    s = g.shape[0]
    spec = pl.BlockSpec((chunk, LANE), lambda c: (c, 0))
    return pl.pallas_call(
        _gate_prep_kernel,
        grid=(s // chunk,),
        in_specs=[spec],
        out_specs=[spec, spec],
        out_shape=[jax.ShapeDtypeStruct((s, LANE), F32)] * 2,
        compiler_params=_params(("parallel",), 32),
        name="gate_prep",
    )(g)


def _conv_kernel(xp_ref, x_ref, xn_ref, w_ref, b_ref, o_ref, xe_ref, *, scale, transpose_out):
    i = pl.program_id(0)
    tb = x_ref.shape[0]
    halo = CONV_W // 2
    xe_ref[0:BF16_ROWS, :] = jnp.where(i > 0, xp_ref[...].astype(F32), 0.0)
    xe_ref[BF16_ROWS:BF16_ROWS + tb, :] = x_ref[...].astype(F32)
    xe_ref[BF16_ROWS + tb:2 * BF16_ROWS + tb, :] = jnp.where(
        i < pl.num_programs(0) - 1, xn_ref[...].astype(F32), 0.0)
    acc = b_ref[...] + w_ref[0:1, :] * xe_ref[pl.ds(BF16_ROWS - halo, tb), :]
    for k in range(1, CONV_W):
        acc = acc + w_ref[k:k + 1, :] * xe_ref[pl.ds(BF16_ROWS - halo + k, tb), :]
    y = acc * _sigmoid(acc)
    if scale != 1.0:
        y = y * scale
    o_ref[...] = (y.T if transpose_out else y).astype(o_ref.dtype)


def _conv_silu(proj, conv_w, conv_b, *, col0, width, scale, transpose_out, tb=512, tc=512):
    s = proj.shape[0]
    rb = tb // BF16_ROWS
    n_rb = s // BF16_ROWS
    cb0 = col0 // tc
    if transpose_out:
        out_spec = pl.BlockSpec((tc, tb), lambda i, j: (j, i))
        out_shape = jax.ShapeDtypeStruct((width, s), BF16)
    else:
        out_spec = pl.BlockSpec((tb, tc), lambda i, j: (i, j))
        out_shape = jax.ShapeDtypeStruct((s, width), BF16)
    return pl.pallas_call(
        functools.partial(_conv_kernel, scale=scale, transpose_out=transpose_out),
        grid=(s // tb, width // tc),
        in_specs=[
            pl.BlockSpec((BF16_ROWS, tc), lambda i, j: (jnp.maximum(i * rb - 1, 0), cb0 + j)),
            pl.BlockSpec((tb, tc), lambda i, j: (i, cb0 + j)),
            pl.BlockSpec((BF16_ROWS, tc), lambda i, j: (jnp.minimum((i + 1) * rb, n_rb - 1), cb0 + j)),
            pl.BlockSpec((CONV_W, tc), lambda i, j: (0, cb0 + j)),
            pl.BlockSpec((1, tc), lambda i, j: (0, cb0 + j)),
        ],
        out_specs=out_spec,
        out_shape=out_shape,
        scratch_shapes=[pltpu.VMEM((tb + 2 * BF16_ROWS, tc), F32)],
        compiler_params=_params(("parallel", "parallel"), 32),
        name="conv_silu_t" if transpose_out else "conv_silu",
    )(proj, proj, proj, conv_w, conv_b)


def _mlstm_direction(q_ref, kt_ref, v_ref, pc_ref, pr_ref, o_ref, c_ref, m_ref, va_ref, *, reverse):
    chunk, dv = v_ref.shape
    q = q_ref[...]
    kt = kt_ref[...]
    va_ref[:, 0:dv] = v_ref[...]
    va = va_ref[...]
    pc = pc_ref[...]
    t0 = 3 if reverse else 0
    cum = jnp.broadcast_to(pc[:, t0 + 1:t0 + 2], (chunk, LANE))
    pmax = jnp.broadcast_to(pc[:, t0 + 2:t0 + 3], (chunk, LANE))
    r0 = 1 if reverse else 0
    b_row = pr_ref[...][r0:r0 + 1, :]
    last = 0 if reverse else chunk - 1
    total = cum[last:last + 1, :]
    b_max = pmax[last:last + 1, :]
    m_prev = m_ref[...]

    m_loc = jnp.maximum(pmax, m_prev)
    ri = lax.broadcasted_iota(jnp.int32, (chunk, chunk), 0)
    ci = lax.broadcasted_iota(jnp.int32, (chunk, chunk), 1)
    mask = (ci >= ri) if reverse else (ci <= ri)
    expo = jnp.where(mask, b_row - _lanes(m_loc, chunk), -jnp.inf)
    w = (jnp.exp(expo) * _dot(q, kt)).astype(BF16)
    a = jnp.exp(m_prev - m_loc)
    q_c = _dot(q, c_ref[...].astype(BF16))
    num_den = _dot(w, va) + _lanes(a, dv + LANE) * q_c
    den = num_den[:, dv:dv + LANE]
    inv = 1.0 / jnp.maximum(jnp.abs(den), jnp.exp(-(cum + m_loc)))
    for jb in range(dv // LANE):
        cols = slice(jb * LANE, (jb + 1) * LANE)
        o_ref[:, cols] = (num_den[:, cols] * inv).astype(o_ref.dtype)

    m_in = jnp.maximum(m_prev, b_max)
    s_in = jnp.exp(b_row - _lanes(m_in, chunk))
    decay = jnp.exp(m_prev - m_in)
    kt_in = (kt.astype(F32) * s_in).astype(BF16)
    c_ref[...] = _lanes(decay, dv + LANE) * c_ref[...] + _dot(kt_in, va)
    m_ref[...] = total + m_in


def _mlstm_kernel(qf, ktf, vf, pcf, prf, qb, ktb, vb, pcb, prb, of, ob, cf, mf, vaf, cb, mb, vab):
    dv = vf.shape[1]

    @pl.when(pl.program_id(1) == 0)
    def _():
        for ref in (cf, mf, cb, mb):
            ref[...] = jnp.zeros(ref.shape, ref.dtype)
        for ref in (vaf, vab):
            ref[:, dv:dv + LANE] = jnp.ones((ref.shape[0], LANE), ref.dtype)

    _mlstm_direction(qf, ktf, vf, pcf, prf, of, cf, mf, vaf, reverse=False)
    _mlstm_direction(qb, ktb, vb, pcb, prb, ob, cb, mb, vab, reverse=True)


def _mlstm_scan(q, kt, proj, p_col, p_row, *, dk, dv, v_col0, chunk):
    s = q.shape[0]
    nc = s // chunk
    v_blk0 = v_col0 // dv
    n_col = p_col.shape[2]
    n_row = p_row.shape[1]

    def fwd(c):
        return c

    def bwd(c):
        return nc - 1 - c

    in_specs = []
    for cmap in (fwd, bwd):
        in_specs += [
            pl.BlockSpec((chunk, dk), lambda h, c, cmap=cmap: (cmap(c), h)),
            pl.BlockSpec((dk, chunk), lambda h, c, cmap=cmap: (h, cmap(c))),
            pl.BlockSpec((chunk, dv), lambda h, c, cmap=cmap: (cmap(c), v_blk0 + h)),
            pl.BlockSpec((None, chunk, n_col), lambda h, c, cmap=cmap: (h, cmap(c), 0)),
            pl.BlockSpec((None, n_row, chunk), lambda h, c, cmap=cmap: (h, 0, cmap(c))),
        ]
    out_specs = [
        pl.BlockSpec((chunk, dv), lambda h, c: (c, h)),
        pl.BlockSpec((chunk, dv), lambda h, c: (nc - 1 - c, h)),
    ]
    state = [pltpu.VMEM((dk, dv + LANE), F32), pltpu.VMEM((1, LANE), F32), pltpu.VMEM((chunk, dv + LANE), BF16)]
    return pl.pallas_call(
        _mlstm_kernel,
        grid=(NH_M, nc),
        in_specs=in_specs,
        out_specs=out_specs,
        out_shape=[jax.ShapeDtypeStruct((s, NH_M * dv), BF16)] * 2,
        scratch_shapes=state + state,
        compiler_params=_params(("parallel", "arbitrary"), 32),
        name="mlstm_scan",
    )(q, kt, proj, p_col, p_row, q, kt, proj, p_col, p_row)


def _rope_table_kernel(pos_ref, freq_ref, sign_ref, cos_ref, sin_ref):
    angle = pos_ref[...].astype(F32) * freq_ref[...]
    cos_ref[...] = jnp.cos(angle)
    sin_ref[...] = jnp.sin(angle) * sign_ref[...]


def _rope_tables(pos_col, freq_full, sign, *, tb=1024):
    s = pos_col.shape[0]
    dk = freq_full.shape[1]
    return pl.pallas_call(
        _rope_table_kernel,
        grid=(s // tb,),
        in_specs=[
            pl.BlockSpec((tb, 1), lambda i: (i, 0)),
            pl.BlockSpec((1, dk), lambda i: (0, 0)),
            pl.BlockSpec((1, dk), lambda i: (0, 0)),
        ],
        out_specs=[pl.BlockSpec((tb, dk), lambda i: (i, 0))] * 2,
        out_shape=[jax.ShapeDtypeStruct((s, dk), F32)] * 2,
        compiler_params=_params(("parallel",), 32),
        name="rope_tables",
    )(pos_col, freq_full, sign)


def _rope_kernel(q_ref, k_ref, cos_ref, sin_ref, qo_ref, kto_ref, *, q_scale):
    dk = q_ref.shape[1]
    cos = cos_ref[...]
    sin = sin_ref[...]
    q = q_ref[...].astype(F32)
    k = k_ref[...].astype(F32)
    q = (q * cos + pltpu.roll(q, dk // 2, axis=1) * sin) * q_scale
    k = k * cos + pltpu.roll(k, dk // 2, axis=1) * sin
    qo_ref[...] = q.astype(qo_ref.dtype)
    kto_ref[...] = k.T.astype(kto_ref.dtype)


def _rope(proj, cos_full, sin_signed, *, dk, q_col0, k_col0, tb=1024):
    s = proj.shape[0]
    q_blk0, k_blk0 = q_col0 // dk, k_col0 // dk
    return pl.pallas_call(
        functools.partial(_rope_kernel, q_scale=dk ** -0.5),
        grid=(s // tb, NH_R),
        in_specs=[
            pl.BlockSpec((tb, dk), lambda i, h: (i, q_blk0 + h)),
            pl.BlockSpec((tb, dk), lambda i, h: (i, k_blk0 + h)),
            pl.BlockSpec((tb, dk), lambda i, h: (i, 0)),
            pl.BlockSpec((tb, dk), lambda i, h: (i, 0)),
        ],
        out_specs=[
            pl.BlockSpec((tb, dk), lambda i, h: (i, h)),
            pl.BlockSpec((dk, tb), lambda i, h: (h, i)),
        ],
        out_shape=[jax.ShapeDtypeStruct((s, NH_R * dk), BF16), jax.ShapeDtypeStruct((NH_R * dk, s), BF16)],
        compiler_params=_params(("parallel", "parallel"), 32),
        name="rope",
    )(proj, proj, cos_full, sin_signed)


def _retention_direction(q_ref, kt_ref, v_ref, o_ref, s_ref, d_ref, qd_ref, kd_ref, cd):
    dv = v_ref.shape[1]
    q = q_ref[...]
    kt = kt_ref[...]
    v = v_ref[...]
    scores = (_dot(q, kt) * d_ref[...]).astype(BF16)
    inter = _dot(q, s_ref[...].astype(BF16))
    o_ref[...] = (_dot(scores, v) + _lanes(qd_ref[...], dv) * inter).astype(o_ref.dtype)
    kt_in = (kt.astype(F32) * kd_ref[...]).astype(BF16)
    s_ref[...] = cd * s_ref[...] + _dot(kt_in, v)


def _retention_kernel(lg_ref, qf, ktf, vf, qb, ktb, vb, of, ob, sf, sb, df, db, qdf, kdf, qdb, kdb):
    chunk = qf.shape[0]
    lg = lg_ref[...]

    @pl.when(pl.program_id(1) == 0)
    def _():
        sf[...] = jnp.zeros(sf.shape, F32)
        sb[...] = jnp.zeros(sb.shape, F32)
        ri = lax.broadcasted_iota(jnp.int32, (chunk, chunk), 0)
        ci = lax.broadcasted_iota(jnp.int32, (chunk, chunk), 1)
        diff = (ri - ci).astype(F32)
        df[...] = jnp.where(diff >= 0.0, jnp.exp(jnp.where(diff >= 0.0, diff, 0.0) * lg), 0.0)
        db[...] = jnp.where(diff < 0.0, jnp.exp(jnp.where(diff < 0.0, -diff, 0.0) * lg), 0.0)
        row = lax.broadcasted_iota(jnp.int32, (chunk, LANE), 0).astype(F32)
        col = lax.broadcasted_iota(jnp.int32, (1, chunk), 1).astype(F32)
        qdf[...] = jnp.exp((row + 1.0) * lg)
        kdf[...] = jnp.exp((chunk - 1.0 - col) * lg)
        qdb[...] = jnp.exp((chunk - row) * lg)
        kdb[...] = jnp.exp(col * lg)

    cd = jnp.exp(chunk * lg)
    _retention_direction(qf, ktf, vf, of, sf, df, qdf, kdf, cd)
    _retention_direction(qb, ktb, vb, ob, sb, db, qdb, kdb, cd)


def _retention_scan(q, kt, proj, log_gamma, *, dk, dv, v_col0, chunk):
    s = q.shape[0]
    nc = s // chunk
    v_blk0 = v_col0 // dv

    def fwd(c):
        return c

    def bwd(c):
        return nc - 1 - c

    in_specs = [pl.BlockSpec((None, 1, 1), lambda h, c: (h, 0, 0))]
    for cmap in (fwd, bwd):
        in_specs += [
            pl.BlockSpec((chunk, dk), lambda h, c, cmap=cmap: (cmap(c), h)),
            pl.BlockSpec((dk, chunk), lambda h, c, cmap=cmap: (h, cmap(c))),
            pl.BlockSpec((chunk, dv), lambda h, c, cmap=cmap: (cmap(c), v_blk0 + h)),
        ]
    out_specs = [
        pl.BlockSpec((chunk, dv), lambda h, c: (c, h)),
        pl.BlockSpec((chunk, dv), lambda h, c: (nc - 1 - c, h)),
    ]
    scratch = [
        pltpu.VMEM((dk, dv), F32), pltpu.VMEM((dk, dv), F32),
        pltpu.VMEM((chunk, chunk), F32), pltpu.VMEM((chunk, chunk), F32),
        pltpu.VMEM((chunk, LANE), F32), pltpu.VMEM((1, chunk), F32),
        pltpu.VMEM((chunk, LANE), F32), pltpu.VMEM((1, chunk), F32),
    ]
    return pl.pallas_call(
        _retention_kernel,
        grid=(NH_R, nc),
        in_specs=in_specs,
        out_specs=out_specs,
        out_shape=[jax.ShapeDtypeStruct((s, NH_R * dv), BF16)] * 2,
        scratch_shapes=scratch,
        compiler_params=_params(("parallel", "arbitrary"), 32),
        name="retention_scan",
    )(log_gamma, q, kt, proj, q, kt, proj)


def _head_norm_gate(h, gain_ref, z_ref, u_ref, n_heads):
    width = h.shape[1] // n_heads
    for hh in range(n_heads):
        cols = slice(hh * width, (hh + 1) * width)
        seg = h[:, cols]
        y = seg * lax.rsqrt(jnp.mean(seg * seg, axis=-1, keepdims=True) + EPS) * gain_ref[:, cols]
        z = z_ref[:, cols].astype(F32)
        u_ref[:, cols] = (y * (z * _sigmoid(z))).astype(BF16)


def _merge_kernel(hmf, hmb, om, zm, hrf, hrb, zr, mw, rw, wpm, wpr, gm, gr, bmix, y_ref, um, ur, *, tn):
    d = y_ref.shape[1]
    h_m = _sigmoid(om[...].astype(F32)) * (hmf[...].astype(F32) + hmb[...].astype(F32))
    _head_norm_gate(h_m, mw, zm, um, NH_M)
    _head_norm_gate(hrf[...].astype(F32) + hrb[...].astype(F32), rw, zr, ur, NH_R)
    for nb in range(d // tn):
        cols = slice(nb * tn, (nb + 1) * tn)
        cols_r = slice(d + nb * tn, d + (nb + 1) * tn)
        mix_m = _sigmoid(gm[:, cols].astype(F32) + bmix[:, cols])
        mix_r = _sigmoid(gr[:, cols].astype(F32) + bmix[:, cols_r])
        y = mix_m * _dot(um[...], wpm[:, cols]) + mix_r * _dot(ur[...], wpr[:, cols])
        y_ref[:, cols] = y.astype(y_ref.dtype)


def _merge(hmf, hmb, hrf, hrb, proj, m_norm_w, r_norm_w, w_proj_m, w_proj_r, b_mix, *,
           om_col0, zm_col0, zr_col0, gm_col0, gr_col0, tm=256, tn=512):
    s, d = hmf.shape
    wide = lambda col0: pl.BlockSpec((tm, d), lambda i: (i, col0 // d))
    row_full = pl.BlockSpec((tm, d), lambda i: (i, 0))
    const = lambda shape: pl.BlockSpec(shape, lambda i: (0, 0), pipeline_mode=pl.Buffered(1))
    return pl.pallas_call(
        functools.partial(_merge_kernel, tn=tn),
        grid=(s // tm,),
        in_specs=[
            row_full, row_full, wide(om_col0), wide(zm_col0),
            row_full, row_full, wide(zr_col0),
            const((1, d)), const((1, d)), const((d, d)), const((d, d)),
            wide(gm_col0), wide(gr_col0), const((1, 2 * d)),
        ],
        out_specs=pl.BlockSpec((tm, d), lambda i: (i, 0)),
        out_shape=jax.ShapeDtypeStruct((s, d), BF16),
        scratch_shapes=[pltpu.VMEM((tm, d), BF16), pltpu.VMEM((tm, d), BF16)],
        compiler_params=_params(("parallel",), 48),
        name="merge",
    )(hmf, hmb, proj, proj, hrf, hrb, proj, m_norm_w, r_norm_w, w_proj_m, w_proj_r, proj, proj, b_mix)


def _outproj_kernel(x_ref, y_ref, w_ref, fw_ref, o_ref, *, final_norm):
    out = x_ref[...] + _dot(y_ref[...], w_ref[...])
    if final_norm:
        ms = jnp.mean(out * out, axis=-1, keepdims=True)
        out = (out * lax.rsqrt(ms + EPS)) * fw_ref[...]
    o_ref[...] = out


def _outproj(x2, y, w_out, final_w, *, final_norm, tm=512):
    s, d = x2.shape
    return pl.pallas_call(
        functools.partial(_outproj_kernel, final_norm=final_norm),
        grid=(s // tm,),
        in_specs=[
            pl.BlockSpec((tm, d), lambda i: (i, 0)),
            pl.BlockSpec((tm, d), lambda i: (i, 0)),
            pl.BlockSpec((d, d), lambda i: (0, 0)),
            pl.BlockSpec((1, d), lambda i: (0, 0)),
        ],
        out_specs=pl.BlockSpec((tm, d), lambda i: (i, 0)),
        out_shape=jax.ShapeDtypeStruct((s, d), F32),
        compiler_params=_params(("parallel",), 48),
        name="outproj",
    )(x2, y, w_out, final_w)


def kernel(x, positions, norm_w, w_in, b_mgate, conv_w, conv_b, m_norm_w, r_norm_w,
           w_proj_m, w_proj_r, b_mix, w_out, final_norm_w):
    batch, s, d = x.shape
    assert batch == 1, "kernel is written for BATCH == 1"
    depth = norm_w.shape[0]
    dqk_m, dv_m = d // 8, d // 4
    dqk_r, dv_r = d // 16, d // 8
    wm_qk, wm_v = NH_M * dqk_m, NH_M * dv_m
    wr_qk, wr_v = NH_R * dqk_r, NH_R * dv_r
    n_gate = N_GATE_TYPES * NH_M
    col_k_m = wm_qk
    col_v_m = 2 * wm_qk
    col_z_m = col_v_m + wm_v
    col_o_m = col_z_m + wm_v
    col_gate = col_o_m + wm_v
    col_q_r = col_gate
    col_k_r = col_q_r + wr_qk
    col_v_r = col_k_r + wr_qk
    col_z_r = col_v_r + wr_v
    col_gate_m = col_z_r + wr_v
    col_gate_r = col_gate_m + d

    x2 = x.reshape(s, d)
    log_gamma = jnp.log1p(-jnp.power(2.0, -5.0 - jnp.arange(NH_R, dtype=F32))).reshape(NH_R, 1, 1)
    inv_freq = jnp.power(ROPE_BASE, -jnp.arange(dqk_r // 2, dtype=F32) / (dqk_r // 2))
    freq_full = jnp.concatenate([inv_freq, inv_freq]).reshape(1, dqk_r)
    sign = jnp.concatenate([-jnp.ones((dqk_r // 2,), F32), jnp.ones((dqk_r // 2,), F32)]).reshape(1, dqk_r)
    cos_full, sin_signed = _rope_tables(positions.reshape(s, 1), freq_full, sign)

    for layer in range(depth):
        w_l = w_in[layer]
        w_main = jnp.concatenate([w_l[:, :col_gate], w_l[:, col_gate + n_gate:]], axis=1).astype(BF16)
        w_gate = jnp.pad(w_l[:, col_gate:col_gate + n_gate], ((0, 0), (0, LANE - n_gate))).astype(BF16)
        b_gate = jnp.pad(b_mgate[layer], (0, LANE - n_gate)).reshape(1, LANE)

        proj, gates = _inproj(x2, norm_w[layer].reshape(1, d), w_main, w_gate, b_gate)

        p1, p2 = _gate_prep(gates, CHUNK_M)
        p1 = p1[:, :n_gate].reshape(s, N_GATE_TYPES, NH_M)
        p2 = p2[:, :n_gate].reshape(s, N_GATE_TYPES, NH_M)
        p_col = jnp.stack([p1[:, 0], p1[:, 1], p2[:, 0], p1[:, 2], p1[:, 3], p2[:, 2]], axis=1).transpose(2, 0, 1)
        p_row = jnp.stack([p1[:, 0], p1[:, 2]], axis=1).transpose(2, 1, 0)

        cw, cb = conv_w[layer], conv_b[layer].reshape(1, 2 * wm_qk)
        q_m = _conv_silu(proj, cw, cb, col0=0, width=wm_qk, scale=dqk_m ** -0.5, transpose_out=False)
        kt_m = _conv_silu(proj, cw, cb, col0=col_k_m, width=wm_qk, scale=1.0, transpose_out=True)
        hmf, hmb = _mlstm_scan(q_m, kt_m, proj, p_col, p_row, dk=dqk_m, dv=dv_m, v_col0=col_v_m, chunk=CHUNK_M)

        q_r, kt_r = _rope(proj, cos_full, sin_signed, dk=dqk_r, q_col0=col_q_r, k_col0=col_k_r)
        hrf, hrb = _retention_scan(q_r, kt_r, proj, log_gamma, dk=dqk_r, dv=dv_r, v_col0=col_v_r, chunk=CHUNK_R)

        y = _merge(hmf, hmb, hrf, hrb, proj, m_norm_w[layer].reshape(1, wm_v), r_norm_w[layer].reshape(1, wr_v),
                   w_proj_m[layer].astype(BF16), w_proj_r[layer].astype(BF16), b_mix[layer].reshape(1, 2 * d),
                   om_col0=col_o_m, zm_col0=col_z_m, zr_col0=col_z_r, gm_col0=col_gate_m, gr_col0=col_gate_r)
        x2 = _outproj(x2, y, w_out[layer].astype(BF16), final_norm_w.reshape(1, d),
                      final_norm=(layer == depth - 1))
    return x2.reshape(batch, s, d)
```

```python
import functools

import jax
import jax.numpy as jnp
from jax import lax
from jax.experimental import pallas as pl
from jax.experimental.pallas import tpu as pltpu

NH_M = 4
NH_R = 8
N_GATE_TYPES = 4
CONV_W = 5
ROPE_BASE = 10000.0
EPS = 1e-6

LANE = 128
BF16_ROWS = 16
MIB = 1024 * 1024

CHUNK_M = 256
CHUNK_R = 256
HEADS_PER_STEP_M = 2
HEADS_PER_STEP_R = 4

F32 = jnp.float32
BF16 = jnp.bfloat16


def _params(semantics, vmem_mib):
    return pltpu.CompilerParams(dimension_semantics=semantics, vmem_limit_bytes=vmem_mib * MIB)


def _sigmoid(x):
    return 1.0 / (1.0 + jnp.exp(-x))


def _lanes(x, width):
    return x if width == LANE else jnp.concatenate([x] * (width // LANE), axis=1)


def _dot(a, b):
    return jnp.dot(a, b, preferred_element_type=F32)


def _inproj_kernel(x_ref, nw_ref, w_ref, wg_ref, bg_ref, o_ref, g_ref, h_scr, *, sub_rows):
    tm = x_ref.shape[0]

    @pl.when(pl.program_id(1) == 0)
    def _():
        def body(r, carry):
            rs = pl.ds(pl.multiple_of(r * sub_rows, sub_rows), sub_rows)
            x = x_ref[rs, :]
            ms = jnp.mean(x * x, axis=-1, keepdims=True)
            h = (x * lax.rsqrt(ms + EPS)) * nw_ref[...]
            h_scr[rs, :] = h.astype(BF16)
            return carry

        lax.fori_loop(0, tm // sub_rows, body, 0)
        g_ref[...] = _dot(h_scr[...], wg_ref[...]) + bg_ref[...]

    o_ref[...] = _dot(h_scr[...], w_ref[...]).astype(o_ref.dtype)


def _inproj(x2, norm_w, w_main, w_gate, b_gate, *, tm=1024, tn=2048):
    s, d = x2.shape
    n = w_main.shape[1]
    return pl.pallas_call(
        functools.partial(_inproj_kernel, sub_rows=256),
        grid=(s // tm, n // tn),
        in_specs=[
            pl.BlockSpec((tm, d), lambda i, j: (i, 0)),
            pl.BlockSpec((1, d), lambda i, j: (0, 0)),
            pl.BlockSpec((d, tn), lambda i, j: (0, j)),
            pl.BlockSpec((d, LANE), lambda i, j: (0, 0)),
            pl.BlockSpec((1, LANE), lambda i, j: (0, 0)),
        ],
        out_specs=[
            pl.BlockSpec((tm, tn), lambda i, j: (i, j)),
            pl.BlockSpec((tm, LANE), lambda i, j: (i, 0)),
        ],
        out_shape=[jax.ShapeDtypeStruct((s, n), BF16), jax.ShapeDtypeStruct((s, LANE), F32)],
        scratch_shapes=[pltpu.VMEM((tm, d), BF16)],
        compiler_params=_params(("parallel", "arbitrary"), 56),
        name="inproj",
    )(x2, norm_w, w_main, w_gate, b_gate)


def _gate_prep_kernel(g_ref, p_ref, pm_ref):
    g = g_ref[...]
    chunk = g.shape[0]
    log_f = jnp.minimum(g, 0.0) - jnp.log1p(jnp.exp(-jnp.abs(g)))
    ri = lax.broadcasted_iota(jnp.int32, (chunk, chunk), 0)
    ci = lax.broadcasted_iota(jnp.int32, (chunk, chunk), 1)
    tril = (ci <= ri).astype(F32)
    triu = (ci >= ri).astype(F32)
    cum_prefix = jnp.dot(tril, log_f, precision=lax.Precision.HIGHEST, preferred_element_type=F32)
    cum_suffix = jnp.dot(triu, log_f, precision=lax.Precision.HIGHEST, preferred_element_type=F32)
    lane = lax.broadcasted_iota(jnp.int32, g.shape, 1)
    row = lax.broadcasted_iota(jnp.int32, g.shape, 0)
    cum = jnp.where(lane < 2 * NH_M, cum_prefix, cum_suffix)
    b = g - pltpu.roll(cum, LANE - NH_M, axis=1)
    p_ref[...] = jnp.where((lane % (2 * NH_M)) < NH_M, b, cum)

    pm_f = b
    pm_b = b
    step = 1
    while step < chunk:
        pm_f = jnp.maximum(pm_f, jnp.where(row >= step, pltpu.roll(pm_f, step, axis=0), -jnp.inf))
        pm_b = jnp.maximum(pm_b, jnp.where(row < chunk - step, pltpu.roll(pm_b, chunk - step, axis=0), -jnp.inf))
        step *= 2
    pm_ref[...] = jnp.where(lane < 2 * NH_M, pm_f, pm_b)


def _gate_prep(g, chunk):
    s = g.shape[0]
    spec = pl.BlockSpec((chunk, LANE), lambda c: (c, 0))
    return pl.pallas_call(
        _gate_prep_kernel,
        grid=(s // chunk,),
        in_specs=[spec],
        out_specs=[spec, spec],
        out_shape=[jax.ShapeDtypeStruct((s, LANE), F32)] * 2,
        compiler_params=_params(("parallel",), 32),
        name="gate_prep",
    )(g)


def _conv_kernel(xp_ref, x_ref, xn_ref, w_ref, b_ref, o_ref, xe_ref, *, scale, transpose_out):
    i = pl.program_id(0)
    tb = x_ref.shape[0]
    halo = CONV_W // 2
    xe_ref[0:BF16_ROWS, :] = jnp.where(i > 0, xp_ref[...].astype(F32), 0.0)
    xe_ref[BF16_ROWS:BF16_ROWS + tb, :] = x_ref[...].astype(F32)
    xe_ref[BF16_ROWS + tb:2 * BF16_ROWS + tb, :] = jnp.where(
        i < pl.num_programs(0) - 1, xn_ref[...].astype(F32), 0.0)
    acc = b_ref[...] + w_ref[0:1, :] * xe_ref[pl.ds(BF16_ROWS - halo, tb), :]
    for k in range(1, CONV_W):
        acc = acc + w_ref[k:k + 1, :] * xe_ref[pl.ds(BF16_ROWS - halo + k, tb), :]
    y = acc * _sigmoid(acc)
    if scale != 1.0:
        y = y * scale
    o_ref[...] = (y.T if transpose_out else y).astype(o_ref.dtype)


def _conv_silu(proj, conv_w, conv_b, *, col0, width, scale, transpose_out, tb=512, tc=512):
    s = proj.shape[0]
    rb = tb // BF16_ROWS
    n_rb = s // BF16_ROWS
    cb0 = col0 // tc
    if transpose_out:
        out_spec = pl.BlockSpec((tc, tb), lambda i, j: (j, i))
        out_shape = jax.ShapeDtypeStruct((width, s), BF16)
    else:
        out_spec = pl.BlockSpec((tb, tc), lambda i, j: (i, j))
        out_shape = jax.ShapeDtypeStruct((s, width), BF16)
    return pl.pallas_call(
        functools.partial(_conv_kernel, scale=scale, transpose_out=transpose_out),
        grid=(s // tb, width // tc),
        in_specs=[
            pl.BlockSpec((BF16_ROWS, tc), lambda i, j: (jnp.maximum(i * rb - 1, 0), cb0 + j)),
            pl.BlockSpec((tb, tc), lambda i, j: (i, cb0 + j)),
            pl.BlockSpec((BF16_ROWS, tc), lambda i, j: (jnp.minimum((i + 1) * rb, n_rb - 1), cb0 + j)),
            pl.BlockSpec((CONV_W, tc), lambda i, j: (0, cb0 + j)),
            pl.BlockSpec((1, tc), lambda i, j: (0, cb0 + j)),
        ],
        out_specs=out_spec,
        out_shape=out_shape,
        scratch_shapes=[pltpu.VMEM((tb + 2 * BF16_ROWS, tc), F32)],
        compiler_params=_params(("parallel", "parallel"), 32),
        name="conv_silu_t" if transpose_out else "conv_silu",
    )(proj, proj, proj, conv_w, conv_b)


def _mlstm_direction(q_ref, kt_ref, v_ref, pc_ref, pr_ref, o_ref, c_ref, m_ref, va_ref, *, reverse):
    chunk, dv = v_ref.shape
    q = q_ref[...]
    kt = kt_ref[...]
    va_ref[:, 0:dv] = v_ref[...]
    va = va_ref[...]
    pc = pc_ref[...]
    t0 = 3 if reverse else 0
    cum = jnp.broadcast_to(pc[:, t0 + 1:t0 + 2], (chunk, LANE))
    pmax = jnp.broadcast_to(pc[:, t0 + 2:t0 + 3], (chunk, LANE))
    r0 = 1 if reverse else 0
    b_row = pr_ref[...][r0:r0 + 1, :]
    last = 0 if reverse else chunk - 1
    total = cum[last:last + 1, :]
    b_max = pmax[last:last + 1, :]
    m_prev = m_ref[...]

    m_loc = jnp.maximum(pmax, m_prev)
    ri = lax.broadcasted_iota(jnp.int32, (chunk, chunk), 0)
    ci = lax.broadcasted_iota(jnp.int32, (chunk, chunk), 1)
    mask = (ci >= ri) if reverse else (ci <= ri)
    expo = jnp.where(mask, b_row - _lanes(m_loc, chunk), -jnp.inf)
    w = (jnp.exp(expo) * _dot(q, kt)).astype(BF16)
    a = jnp.exp(m_prev - m_loc)
    q_c = _dot(q, c_ref[...].astype(BF16))
    num_den = _dot(w, va) + _lanes(a, dv + LANE) * q_c
    den = num_den[:, dv:dv + LANE]
    inv = 1.0 / jnp.maximum(jnp.abs(den), jnp.exp(-(cum + m_loc)))
    for jb in range(dv // LANE):
        cols = slice(jb * LANE, (jb + 1) * LANE)
        o_ref[:, cols] = (num_den[:, cols] * inv).astype(o_ref.dtype)

    m_in = jnp.maximum(m_prev, b_max)
    s_in = jnp.exp(b_row - _lanes(m_in, chunk))
    decay = jnp.exp(m_prev - m_in)
    kt_in = (kt.astype(F32) * s_in).astype(BF16)
    c_ref[...] = _lanes(decay, dv + LANE) * c_ref[...] + _dot(kt_in, va)
    m_ref[...] = total + m_in


def _mlstm_kernel(qf, ktf, vf, pcf, prf, qb, ktb, vb, pcb, prb, of, ob, *scratch, heads, dk, dv):
    fwd_state, bwd_state = scratch[:3 * heads], scratch[3 * heads:]

    @pl.when(pl.program_id(1) == 0)
    def _():
        for hh in range(2 * heads):
            c_ref, m_ref, va_ref = scratch[3 * hh:3 * hh + 3]
            c_ref[...] = jnp.zeros(c_ref.shape, c_ref.dtype)
            m_ref[...] = jnp.zeros(m_ref.shape, m_ref.dtype)
            va_ref[:, dv:dv + LANE] = jnp.ones((va_ref.shape[0], LANE), va_ref.dtype)

    for hh in range(heads):
        qk_cols = pl.ds(hh * dk, dk)
        v_cols = pl.ds(hh * dv, dv)
        _mlstm_direction(qf.at[:, qk_cols], ktf.at[qk_cols, :], vf.at[:, v_cols], pcf.at[hh], prf.at[hh],
                         of.at[:, v_cols], *fwd_state[3 * hh:3 * hh + 3], reverse=False)
        _mlstm_direction(qb.at[:, qk_cols], ktb.at[qk_cols, :], vb.at[:, v_cols], pcb.at[hh], prb.at[hh],
                         ob.at[:, v_cols], *bwd_state[3 * hh:3 * hh + 3], reverse=True)


def _mlstm_scan(q, kt, proj, p_col, p_row, *, dk, dv, v_col0, chunk, heads=HEADS_PER_STEP_M):
    s = q.shape[0]
    nc = s // chunk
    v_blk0 = v_col0 // (heads * dv)
    n_col = p_col.shape[2]
    n_row = p_row.shape[1]

    def fwd(c):
        return c

    def bwd(c):
        return nc - 1 - c

    in_specs = []
    for cmap in (fwd, bwd):
        in_specs += [
            pl.BlockSpec((chunk, heads * dk), lambda h, c, cmap=cmap: (cmap(c), h)),
            pl.BlockSpec((heads * dk, chunk), lambda h, c, cmap=cmap: (h, cmap(c))),
            pl.BlockSpec((chunk, heads * dv), lambda h, c, cmap=cmap: (cmap(c), v_blk0 + h)),
            pl.BlockSpec((heads, chunk, n_col), lambda h, c, cmap=cmap: (h, cmap(c), 0)),
            pl.BlockSpec((heads, n_row, chunk), lambda h, c, cmap=cmap: (h, 0, cmap(c))),
        ]
    out_specs = [
        pl.BlockSpec((chunk, heads * dv), lambda h, c: (c, h)),
        pl.BlockSpec((chunk, heads * dv), lambda h, c: (nc - 1 - c, h)),
    ]
    state = [pltpu.VMEM((dk, dv + LANE), F32), pltpu.VMEM((1, LANE), F32), pltpu.VMEM((chunk, dv + LANE), BF16)]
    return pl.pallas_call(
        functools.partial(_mlstm_kernel, heads=heads, dk=dk, dv=dv),
        grid=(NH_M // heads, nc),
        in_specs=in_specs,
        out_specs=out_specs,
        out_shape=[jax.ShapeDtypeStruct((s, NH_M * dv), BF16)] * 2,
        scratch_shapes=state * (2 * heads),
        compiler_params=_params(("parallel", "arbitrary"), 40),
        name="mlstm_scan",
    )(q, kt, proj, p_col, p_row, q, kt, proj, p_col, p_row)


def _rope_table_kernel(pos_ref, freq_ref, sign_ref, cos_ref, sin_ref):
    angle = pos_ref[...].astype(F32) * freq_ref[...]
    cos_ref[...] = jnp.cos(angle)
    sin_ref[...] = jnp.sin(angle) * sign_ref[...]


def _rope_tables(pos_col, freq_full, sign, *, tb=1024):
    s = pos_col.shape[0]
    dk = freq_full.shape[1]
    return pl.pallas_call(
        _rope_table_kernel,
        grid=(s // tb,),
        in_specs=[
            pl.BlockSpec((tb, 1), lambda i: (i, 0)),
            pl.BlockSpec((1, dk), lambda i: (0, 0)),
            pl.BlockSpec((1, dk), lambda i: (0, 0)),
        ],
        out_specs=[pl.BlockSpec((tb, dk), lambda i: (i, 0))] * 2,
        out_shape=[jax.ShapeDtypeStruct((s, dk), F32)] * 2,
        compiler_params=_params(("parallel",), 32),
        name="rope_tables",
    )(pos_col, freq_full, sign)


def _rope_kernel(q_ref, k_ref, cos_ref, sin_ref, qo_ref, kto_ref, *, q_scale):
    dk = q_ref.shape[1]
    cos = cos_ref[...]
    sin = sin_ref[...]
    q = q_ref[...].astype(F32)
    k = k_ref[...].astype(F32)
    q = (q * cos + pltpu.roll(q, dk // 2, axis=1) * sin) * q_scale
    k = k * cos + pltpu.roll(k, dk // 2, axis=1) * sin
    qo_ref[...] = q.astype(qo_ref.dtype)
    kto_ref[...] = k.T.astype(kto_ref.dtype)


def _rope(proj, cos_full, sin_signed, *, dk, q_col0, k_col0, tb=1024):
    s = proj.shape[0]
    q_blk0, k_blk0 = q_col0 // dk, k_col0 // dk
    return pl.pallas_call(
        functools.partial(_rope_kernel, q_scale=dk ** -0.5),
        grid=(s // tb, NH_R),
        in_specs=[
            pl.BlockSpec((tb, dk), lambda i, h: (i, q_blk0 + h)),
            pl.BlockSpec((tb, dk), lambda i, h: (i, k_blk0 + h)),
            pl.BlockSpec((tb, dk), lambda i, h: (i, 0)),
            pl.BlockSpec((tb, dk), lambda i, h: (i, 0)),
        ],
        out_specs=[
            pl.BlockSpec((tb, dk), lambda i, h: (i, h)),
            pl.BlockSpec((dk, tb), lambda i, h: (h, i)),
        ],
        out_shape=[jax.ShapeDtypeStruct((s, NH_R * dk), BF16), jax.ShapeDtypeStruct((NH_R * dk, s), BF16)],
        compiler_params=_params(("parallel", "parallel"), 32),
        name="rope",
    )(proj, proj, cos_full, sin_signed)


def _retention_direction(q_ref, kt_ref, v_ref, o_ref, s_ref, d_ref, qd_ref, kd_ref, cd):
    dv = v_ref.shape[1]
    q = q_ref[...]
    kt = kt_ref[...]
    v = v_ref[...]
    scores = (_dot(q, kt) * d_ref[...]).astype(BF16)
    inter = _dot(q, s_ref[...].astype(BF16))
    o_ref[...] = (_dot(scores, v) + _lanes(qd_ref[...], dv) * inter).astype(o_ref.dtype)
    kt_in = (kt.astype(F32) * kd_ref[...]).astype(BF16)
    s_ref[...] = cd * s_ref[...] + _dot(kt_in, v)


def _retention_kernel(lg_ref, qf, ktf, vf, qb, ktb, vb, of, ob, *scratch, heads, dk, dv):
    chunk = qf.shape[0]
    first = pl.program_id(1) == 0
    for hh in range(heads):
        sf, sb, df, db, qdf, kdf, qdb, kdb = scratch[8 * hh:8 * hh + 8]
        lg = lg_ref[hh]

        @pl.when(first)
        def _():
            sf[...] = jnp.zeros(sf.shape, F32)
            sb[...] = jnp.zeros(sb.shape, F32)
            ri = lax.broadcasted_iota(jnp.int32, (chunk, chunk), 0)
            ci = lax.broadcasted_iota(jnp.int32, (chunk, chunk), 1)
            diff = (ri - ci).astype(F32)
            df[...] = jnp.where(diff >= 0.0, jnp.exp(jnp.where(diff >= 0.0, diff, 0.0) * lg), 0.0)
            db[...] = jnp.where(diff < 0.0, jnp.exp(jnp.where(diff < 0.0, -diff, 0.0) * lg), 0.0)
            row = lax.broadcasted_iota(jnp.int32, (chunk, LANE), 0).astype(F32)
            col = lax.broadcasted_iota(jnp.int32, (1, chunk), 1).astype(F32)
            qdf[...] = jnp.exp((row + 1.0) * lg)
            kdf[...] = jnp.exp((chunk - 1.0 - col) * lg)
            qdb[...] = jnp.exp((chunk - row) * lg)
            kdb[...] = jnp.exp(col * lg)

    for hh in range(heads):
        sf, sb, df, db, qdf, kdf, qdb, kdb = scratch[8 * hh:8 * hh + 8]
        cd = jnp.exp(chunk * lg_ref[hh])
        qk_cols = pl.ds(hh * dk, dk)
        v_cols = pl.ds(hh * dv, dv)
        _retention_direction(qf.at[:, qk_cols], ktf.at[qk_cols, :], vf.at[:, v_cols], of.at[:, v_cols],
                             sf, df, qdf, kdf, cd)
        _retention_direction(qb.at[:, qk_cols], ktb.at[qk_cols, :], vb.at[:, v_cols], ob.at[:, v_cols],
                             sb, db, qdb, kdb, cd)


def _retention_scan(q, kt, proj, log_gamma, *, dk, dv, v_col0, chunk, heads=HEADS_PER_STEP_R):
    s = q.shape[0]
    nc = s // chunk
    v_blk0 = v_col0 // (heads * dv)

    def fwd(c):
        return c

    def bwd(c):
        return nc - 1 - c

    in_specs = [pl.BlockSpec((heads, 1, 1), lambda h, c: (h, 0, 0))]
    for cmap in (fwd, bwd):
        in_specs += [
            pl.BlockSpec((chunk, heads * dk), lambda h, c, cmap=cmap: (cmap(c), h)),
            pl.BlockSpec((heads * dk, chunk), lambda h, c, cmap=cmap: (h, cmap(c))),
            pl.BlockSpec((chunk, heads * dv), lambda h, c, cmap=cmap: (cmap(c), v_blk0 + h)),
        ]
    out_specs = [
        pl.BlockSpec((chunk, heads * dv), lambda h, c: (c, h)),
        pl.BlockSpec((chunk, heads * dv), lambda h, c: (nc - 1 - c, h)),
    ]
    per_head = [
        pltpu.VMEM((dk, dv), F32), pltpu.VMEM((dk, dv), F32),
        pltpu.VMEM((chunk, chunk), F32), pltpu.VMEM((chunk, chunk), F32),
        pltpu.VMEM((chunk, LANE), F32), pltpu.VMEM((1, chunk), F32),
        pltpu.VMEM((chunk, LANE), F32), pltpu.VMEM((1, chunk), F32),
    ]
    return pl.pallas_call(
        functools.partial(_retention_kernel, heads=heads, dk=dk, dv=dv),
        grid=(NH_R // heads, nc),
        in_specs=in_specs,
        out_specs=out_specs,
        out_shape=[jax.ShapeDtypeStruct((s, NH_R * dv), BF16)] * 2,
        scratch_shapes=per_head * heads,
        compiler_params=_params(("parallel", "arbitrary"), 40),
        name="retention_scan",
    )(log_gamma, q, kt, proj, q, kt, proj)


def _head_norm_gate(h, gain_ref, z_ref, u_ref, n_heads):
    width = h.shape[1] // n_heads
    for hh in range(n_heads):
        cols = slice(hh * width, (hh + 1) * width)
        seg = h[:, cols]
        y = seg * lax.rsqrt(jnp.mean(seg * seg, axis=-1, keepdims=True) + EPS) * gain_ref[:, cols]
        z = z_ref[:, cols].astype(F32)
        u_ref[:, cols] = (y * (z * _sigmoid(z))).astype(BF16)


def _merge_kernel(hmf, hmb, om, zm, hrf, hrb, zr, mw, rw, wpm, wpr, gm, gr, bmix, y_ref, um, ur, *, tn):
    d = y_ref.shape[1]
    h_m = _sigmoid(om[...].astype(F32)) * (hmf[...].astype(F32) + hmb[...].astype(F32))
    _head_norm_gate(h_m, mw, zm, um, NH_M)
    _head_norm_gate(hrf[...].astype(F32) + hrb[...].astype(F32), rw, zr, ur, NH_R)
    for nb in range(d // tn):
        cols = slice(nb * tn, (nb + 1) * tn)
        cols_r = slice(d + nb * tn, d + (nb + 1) * tn)
        mix_m = _sigmoid(gm[:, cols].astype(F32) + bmix[:, cols])
        mix_r = _sigmoid(gr[:, cols].astype(F32) + bmix[:, cols_r])
        y = mix_m * _dot(um[...], wpm[:, cols]) + mix_r * _dot(ur[...], wpr[:, cols])
        y_ref[:, cols] = y.astype(y_ref.dtype)


def _merge(hmf, hmb, hrf, hrb, proj, m_norm_w, r_norm_w, w_proj_m, w_proj_r, b_mix, *,
           om_col0, zm_col0, zr_col0, gm_col0, gr_col0, tm=256, tn=512):
    s, d = hmf.shape
    wide = lambda col0: pl.BlockSpec((tm, d), lambda i: (i, col0 // d))
    row_full = pl.BlockSpec((tm, d), lambda i: (i, 0))
    const = lambda shape: pl.BlockSpec(shape, lambda i: (0, 0), pipeline_mode=pl.Buffered(1))
    return pl.pallas_call(
        functools.partial(_merge_kernel, tn=tn),
        grid=(s // tm,),
        in_specs=[
            row_full, row_full, wide(om_col0), wide(zm_col0),
            row_full, row_full, wide(zr_col0),
            const((1, d)), const((1, d)), const((d, d)), const((d, d)),
            wide(gm_col0), wide(gr_col0), const((1, 2 * d)),
        ],
        out_specs=pl.BlockSpec((tm, d), lambda i: (i, 0)),
        out_shape=jax.ShapeDtypeStruct((s, d), BF16),
        scratch_shapes=[pltpu.VMEM((tm, d), BF16), pltpu.VMEM((tm, d), BF16)],
        compiler_params=_params(("parallel",), 48),
        name="merge",
    )(hmf, hmb, proj, proj, hrf, hrb, proj, m_norm_w, r_norm_w, w_proj_m, w_proj_r, proj, proj, b_mix)


def _outproj_kernel(x_ref, y_ref, w_ref, fw_ref, o_ref, *, final_norm):
    out = x_ref[...] + _dot(y_ref[...], w_ref[...])
    if final_norm:
        ms = jnp.mean(out * out, axis=-1, keepdims=True)
        out = (out * lax.rsqrt(ms + EPS)) * fw_ref[...]
    o_ref[...] = out


def _outproj(x2, y, w_out, final_w, *, final_norm, tm=512):
    s, d = x2.shape
    return pl.pallas_call(
        functools.partial(_outproj_kernel, final_norm=final_norm),
        grid=(s // tm,),
        in_specs=[
            pl.BlockSpec((tm, d), lambda i: (i, 0)),
            pl.BlockSpec((tm, d), lambda i: (i, 0)),
            pl.BlockSpec((d, d), lambda i: (0, 0)),
            pl.BlockSpec((1, d), lambda i: (0, 0)),
        ],
        out_specs=pl.BlockSpec((tm, d), lambda i: (i, 0)),
        out_shape=jax.ShapeDtypeStruct((s, d), F32),
        compiler_params=_params(("parallel",), 48),
        name="outproj",
    )(x2, y, w_out, final_w)


def kernel(x, positions, norm_w, w_in, b_mgate, conv_w, conv_b, m_norm_w, r_norm_w,
           w_proj_m, w_proj_r, b_mix, w_out, final_norm_w):
    batch, s, d = x.shape
    assert batch == 1, "kernel is written for BATCH == 1"
    depth = norm_w.shape[0]
    dqk_m, dv_m = d // 8, d // 4
    dqk_r, dv_r = d // 16, d // 8
    wm_qk, wm_v = NH_M * dqk_m, NH_M * dv_m
    wr_qk, wr_v = NH_R * dqk_r, NH_R * dv_r
    n_gate = N_GATE_TYPES * NH_M
    col_k_m = wm_qk
    col_v_m = 2 * wm_qk
    col_z_m = col_v_m + wm_v
    col_o_m = col_z_m + wm_v
    col_gate = col_o_m + wm_v
    col_q_r = col_gate
    col_k_r = col_q_r + wr_qk
    col_v_r = col_k_r + wr_qk
    col_z_r = col_v_r + wr_v
    col_gate_m = col_z_r + wr_v
    col_gate_r = col_gate_m + d

    x2 = x.reshape(s, d)
    log_gamma = jnp.log1p(-jnp.power(2.0, -5.0 - jnp.arange(NH_R, dtype=F32))).reshape(NH_R, 1, 1)
    inv_freq = jnp.power(ROPE_BASE, -jnp.arange(dqk_r // 2, dtype=F32) / (dqk_r // 2))
    freq_full = jnp.concatenate([inv_freq, inv_freq]).reshape(1, dqk_r)
    sign = jnp.concatenate([-jnp.ones((dqk_r // 2,), F32), jnp.ones((dqk_r // 2,), F32)]).reshape(1, dqk_r)
    cos_full, sin_signed = _rope_tables(positions.reshape(s, 1), freq_full, sign)

    for layer in range(depth):
        w_l = w_in[layer]
        w_main = jnp.concatenate([w_l[:, :col_gate], w_l[:, col_gate + n_gate:]], axis=1).astype(BF16)
        w_gate = jnp.pad(w_l[:, col_gate:col_gate + n_gate], ((0, 0), (0, LANE - n_gate))).astype(BF16)
        b_gate = jnp.pad(b_mgate[layer], (0, LANE - n_gate)).reshape(1, LANE)

        proj, gates = _inproj(x2, norm_w[layer].reshape(1, d), w_main, w_gate, b_gate)

        p1, p2 = _gate_prep(gates, CHUNK_M)
        p1 = p1[:, :n_gate].reshape(s, N_GATE_TYPES, NH_M)
        p2 = p2[:, :n_gate].reshape(s, N_GATE_TYPES, NH_M)
        p_col = jnp.stack([p1[:, 0], p1[:, 1], p2[:, 0], p1[:, 2], p1[:, 3], p2[:, 2]], axis=1).transpose(2, 0, 1)
        p_row = jnp.stack([p1[:, 0], p1[:, 2]], axis=1).transpose(2, 1, 0)

        cw, cb = conv_w[layer], conv_b[layer].reshape(1, 2 * wm_qk)
        q_m = _conv_silu(proj, cw, cb, col0=0, width=wm_qk, scale=dqk_m ** -0.5, transpose_out=False)
        kt_m = _conv_silu(proj, cw, cb, col0=col_k_m, width=wm_qk, scale=1.0, transpose_out=True)
        hmf, hmb = _mlstm_scan(q_m, kt_m, proj, p_col, p_row, dk=dqk_m, dv=dv_m, v_col0=col_v_m, chunk=CHUNK_M)

        q_r, kt_r = _rope(proj, cos_full, sin_signed, dk=dqk_r, q_col0=col_q_r, k_col0=col_k_r)
        hrf, hrb = _retention_scan(q_r, kt_r, proj, log_gamma, dk=dqk_r, dv=dv_r, v_col0=col_v_r, chunk=CHUNK_R)

        y = _merge(hmf, hmb, hrf, hrb, proj, m_norm_w[layer].reshape(1, wm_v), r_norm_w[layer].reshape(1, wr_v),
                   w_proj_m[layer].astype(BF16), w_proj_r[layer].astype(BF16), b_mix[layer].reshape(1, 2 * d),
                   om_col0=col_o_m, zm_col0=col_z_m, zr_col0=col_z_r, gm_col0=col_gate_m, gr_col0=col_gate_r)
        x2 = _outproj(x2, y, w_out[layer].astype(BF16), final_norm_w.reshape(1, d),
                      final_norm=(layer == depth - 1))
    return x2.reshape(batch, s, d)
```

```python
import functools

import jax
import jax.numpy as jnp
from jax import lax
from jax.experimental import pallas as pl
from jax.experimental.pallas import tpu as pltpu

NH_M = 4
NH_R = 8
N_GATE_TYPES = 4
CONV_W = 5
ROPE_BASE = 10000.0
EPS = 1e-6

LANE = 128
BF16_ROWS = 16
MIB = 1024 * 1024

CHUNK_M = 256
CHUNK_R = 256
HEADS_PER_STEP_R = 4

F32 = jnp.float32
BF16 = jnp.bfloat16


def _params(semantics, vmem_mib):
    return pltpu.CompilerParams(dimension_semantics=semantics, vmem_limit_bytes=vmem_mib * MIB)


def _sigmoid(x):
    return 0.5 * jnp.tanh(0.5 * x) + 0.5


def _silu(x):
    half = 0.5 * x
    return half + half * jnp.tanh(half)


def _lanes(x, width):
    return x if width == LANE else jnp.concatenate([x] * (width // LANE), axis=1)


def _dot(a, b):
    return jnp.dot(a, b, preferred_element_type=F32)


def _inproj_kernel(x_ref, nw_ref, w_ref, wg_ref, bg_ref, o_ref, g_ref, h_scr, *, sub_rows):
    tm = x_ref.shape[0]

    @pl.when(pl.program_id(1) == 0)
    def _():
        def body(r, carry):
            rs = pl.ds(pl.multiple_of(r * sub_rows, sub_rows), sub_rows)
            x = x_ref[rs, :]
            ms = jnp.mean(x * x, axis=-1, keepdims=True)
            h = (x * lax.rsqrt(ms + EPS)) * nw_ref[...]
            h_scr[rs, :] = h.astype(BF16)
            return carry

        lax.fori_loop(0, tm // sub_rows, body, 0)
        g_ref[...] = _dot(h_scr[...], wg_ref[...]) + bg_ref[...]

    o_ref[...] = _dot(h_scr[...], w_ref[...]).astype(o_ref.dtype)


def _inproj(x2, norm_w, w_main, w_gate, b_gate, *, tm=1024, tn=2048):
    s, d = x2.shape
    n = w_main.shape[1]
    return pl.pallas_call(
        functools.partial(_inproj_kernel, sub_rows=256),
        grid=(s // tm, n // tn),
        in_specs=[
            pl.BlockSpec((tm, d), lambda i, j: (i, 0)),
            pl.BlockSpec((1, d), lambda i, j: (0, 0)),
            pl.BlockSpec((d, tn), lambda i, j: (0, j)),
            pl.BlockSpec((d, LANE), lambda i, j: (0, 0)),
            pl.BlockSpec((1, LANE), lambda i, j: (0, 0)),
        ],
        out_specs=[
            pl.BlockSpec((tm, tn), lambda i, j: (i, j)),
            pl.BlockSpec((tm, LANE), lambda i, j: (i, 0)),
        ],
        out_shape=[jax.ShapeDtypeStruct((s, n), BF16), jax.ShapeDtypeStruct((s, LANE), F32)],
        scratch_shapes=[pltpu.VMEM((tm, d), BF16)],
        compiler_params=_params(("parallel", "arbitrary"), 56),
        name="inproj",
    )(x2, norm_w, w_main, w_gate, b_gate)


def _gate_prep_kernel(g_ref, p_ref, pm_ref, pt_ref):
    g = g_ref[...]
    chunk = g.shape[0]
    log_f = jnp.minimum(g, 0.0) - jnp.log1p(jnp.exp(-jnp.abs(g)))
    ri = lax.broadcasted_iota(jnp.int32, (chunk, chunk), 0)
    ci = lax.broadcasted_iota(jnp.int32, (chunk, chunk), 1)
    tril = (ci <= ri).astype(F32)
    triu = (ci >= ri).astype(F32)
    cum_prefix = jnp.dot(tril, log_f, precision=lax.Precision.HIGHEST, preferred_element_type=F32)
    cum_suffix = jnp.dot(triu, log_f, precision=lax.Precision.HIGHEST, preferred_element_type=F32)
    lane = lax.broadcasted_iota(jnp.int32, g.shape, 1)
    row = lax.broadcasted_iota(jnp.int32, g.shape, 0)
    cum = jnp.where(lane < 2 * NH_M, cum_prefix, cum_suffix)
    b = g - pltpu.roll(cum, LANE - NH_M, axis=1)
    p = jnp.where((lane % (2 * NH_M)) < NH_M, b, cum)
    p_ref[...] = p
    pt_ref[...] = p.T[:pt_ref.shape[0], :]

    pm_f = b
    pm_b = b
    step = 1
    while step < chunk:
        pm_f = jnp.maximum(pm_f, jnp.where(row >= step, pltpu.roll(pm_f, step, axis=0), -jnp.inf))
        pm_b = jnp.maximum(pm_b, jnp.where(row < chunk - step, pltpu.roll(pm_b, chunk - step, axis=0), -jnp.inf))
        step *= 2
    pm_ref[...] = jnp.where(lane < 2 * NH_M, pm_f, pm_b)


def _gate_prep(g, chunk):
    s = g.shape[0]
    n_rows = N_GATE_TYPES * NH_M
    spec = pl.BlockSpec((chunk, LANE), lambda c: (c, 0))
    return pl.pallas_call(
        _gate_prep_kernel,
        grid=(s // chunk,),
        in_specs=[spec],
        out_specs=[spec, spec, pl.BlockSpec((n_rows, chunk), lambda c: (0, c))],
        out_shape=[jax.ShapeDtypeStruct((s, LANE), F32)] * 2 + [jax.ShapeDtypeStruct((n_rows, s), F32)],
        compiler_params=_params(("parallel",), 32),
        name="gate_prep",
    )(g)


def _conv_kernel(xp_ref, x_ref, xn_ref, w_ref, b_ref, o_ref, xe_ref, *, scale, transpose_out):
    i = pl.program_id(0)
    tb = x_ref.shape[0]
    halo = CONV_W // 2
    xe_ref[0:BF16_ROWS, :] = jnp.where(i > 0, xp_ref[...].astype(F32), 0.0)
    xe_ref[BF16_ROWS:BF16_ROWS + tb, :] = x_ref[...].astype(F32)
    xe_ref[BF16_ROWS + tb:2 * BF16_ROWS + tb, :] = jnp.where(
        i < pl.num_programs(0) - 1, xn_ref[...].astype(F32), 0.0)
    acc = b_ref[...] + w_ref[0:1, :] * xe_ref[pl.ds(BF16_ROWS - halo, tb), :]
    for k in range(1, CONV_W):
        acc = acc + w_ref[k:k + 1, :] * xe_ref[pl.ds(BF16_ROWS - halo + k, tb), :]
    y = _silu(acc)
    if scale != 1.0:
        y = y * scale
    o_ref[...] = (y.T if transpose_out else y).astype(o_ref.dtype)


def _conv_silu(proj, conv_w, conv_b, *, col0, width, scale, transpose_out, tb=512, tc=512):
    s = proj.shape[0]
    rb = tb // BF16_ROWS
    n_rb = s // BF16_ROWS
    cb0 = col0 // tc
    if transpose_out:
        out_spec = pl.BlockSpec((tc, tb), lambda i, j: (j, i))
        out_shape = jax.ShapeDtypeStruct((width, s), BF16)
    else:
        out_spec = pl.BlockSpec((tb, tc), lambda i, j: (i, j))
        out_shape = jax.ShapeDtypeStruct((s, width), BF16)
    return pl.pallas_call(
        functools.partial(_conv_kernel, scale=scale, transpose_out=transpose_out),
        grid=(s // tb, width // tc),
        in_specs=[
            pl.BlockSpec((BF16_ROWS, tc), lambda i, j: (jnp.maximum(i * rb - 1, 0), cb0 + j)),
            pl.BlockSpec((tb, tc), lambda i, j: (i, cb0 + j)),
            pl.BlockSpec((BF16_ROWS, tc), lambda i, j: (jnp.minimum((i + 1) * rb, n_rb - 1), cb0 + j)),
            pl.BlockSpec((CONV_W, tc), lambda i, j: (0, cb0 + j)),
            pl.BlockSpec((1, tc), lambda i, j: (0, cb0 + j)),
        ],
        out_specs=out_spec,
        out_shape=out_shape,
        scratch_shapes=[pltpu.VMEM((tb + 2 * BF16_ROWS, tc), F32)],
        compiler_params=_params(("parallel", "parallel"), 32),
        name="conv_silu_t" if transpose_out else "conv_silu",
    )(proj, proj, proj, conv_w, conv_b)


def _mlstm_direction(q_ref, kt_ref, v_ref, p_ref, pm_ref, pt_ref, o_ref, c_ref, m_ref, va_ref, *, head, reverse):
    chunk, dv = v_ref.shape
    q = q_ref[...]
    kt = kt_ref[...]
    va_ref[:, 0:dv] = v_ref[...]
    va = va_ref[...]
    lane_b = (2 * NH_M if reverse else 0) + head
    lane_cum = lane_b + NH_M
    cum = jnp.broadcast_to(p_ref[:, lane_cum:lane_cum + 1], (chunk, LANE))
    pmax = jnp.broadcast_to(pm_ref[:, lane_b:lane_b + 1], (chunk, LANE))
    b_row = pt_ref[lane_b:lane_b + 1, :]
    last = 0 if reverse else chunk - 1
    total = cum[last:last + 1, :]
    b_max = pmax[last:last + 1, :]
    m_prev = m_ref[...]

    m_loc = jnp.maximum(pmax, m_prev)
    ri = lax.broadcasted_iota(jnp.int32, (chunk, chunk), 0)
    ci = lax.broadcasted_iota(jnp.int32, (chunk, chunk), 1)
    mask = (ci >= ri) if reverse else (ci <= ri)
    expo = jnp.where(mask, b_row - _lanes(m_loc, chunk), -jnp.inf)
    w = (jnp.exp(expo) * _dot(q, kt)).astype(BF16)
    a = jnp.exp(m_prev - m_loc)
    q_c = _dot(q, c_ref[...].astype(BF16))
    num_den = _dot(w, va) + _lanes(a, dv + LANE) * q_c
    den = num_den[:, dv:dv + LANE]
    inv = 1.0 / jnp.maximum(jnp.abs(den), jnp.exp(-(cum + m_loc)))
    for jb in range(dv // LANE):
        cols = slice(jb * LANE, (jb + 1) * LANE)
        o_ref[:, cols] = (num_den[:, cols] * inv).astype(o_ref.dtype)

    m_in = jnp.maximum(m_prev, b_max)
    s_in = jnp.exp(b_row - _lanes(m_in, chunk))
    decay = jnp.exp(m_prev - m_in)
    kt_in = (kt.astype(F32) * s_in).astype(BF16)
    c_ref[...] = _lanes(decay, dv + LANE) * c_ref[...] + _dot(kt_in, va)
    m_ref[...] = total + m_in


def _mlstm_kernel(qf, ktf, vf, pf, pmf, ptf, qb, ktb, vb, pb, pmb, ptb, of, ob, *scratch, heads, dk, dv):
    fwd_state, bwd_state = scratch[:3 * heads], scratch[3 * heads:]

    @pl.when(pl.program_id(1) == 0)
    def _():
        for hh in range(2 * heads):
            c_ref, m_ref, va_ref = scratch[3 * hh:3 * hh + 3]
            c_ref[...] = jnp.zeros(c_ref.shape, c_ref.dtype)
            m_ref[...] = jnp.zeros(m_ref.shape, m_ref.dtype)
            va_ref[:, dv:dv + LANE] = jnp.ones((va_ref.shape[0], LANE), va_ref.dtype)

    for hh in range(heads):
        qk_cols = pl.ds(hh * dk, dk)
        v_cols = pl.ds(hh * dv, dv)
        _mlstm_direction(qf.at[:, qk_cols], ktf.at[qk_cols, :], vf.at[:, v_cols], pf, pmf, ptf,
                         of.at[:, v_cols], *fwd_state[3 * hh:3 * hh + 3], head=hh, reverse=False)
        _mlstm_direction(qb.at[:, qk_cols], ktb.at[qk_cols, :], vb.at[:, v_cols], pb, pmb, ptb,
                         ob.at[:, v_cols], *bwd_state[3 * hh:3 * hh + 3], head=hh, reverse=True)


def _mlstm_scan(q, kt, proj, p, pm, pt, *, dk, dv, v_col0, chunk):
    heads = NH_M
    s = q.shape[0]
    nc = s // chunk
    v_blk0 = v_col0 // (heads * dv)
    n_row = pt.shape[0]

    def fwd(c):
        return c

    def bwd(c):
        return nc - 1 - c

    in_specs = []
    for cmap in (fwd, bwd):
        in_specs += [
            pl.BlockSpec((chunk, heads * dk), lambda h, c, cmap=cmap: (cmap(c), h)),
            pl.BlockSpec((heads * dk, chunk), lambda h, c, cmap=cmap: (h, cmap(c))),
            pl.BlockSpec((chunk, heads * dv), lambda h, c, cmap=cmap: (cmap(c), v_blk0 + h)),
            pl.BlockSpec((chunk, LANE), lambda h, c, cmap=cmap: (cmap(c), 0)),
            pl.BlockSpec((chunk, LANE), lambda h, c, cmap=cmap: (cmap(c), 0)),
            pl.BlockSpec((n_row, chunk), lambda h, c, cmap=cmap: (0, cmap(c))),
        ]
    out_specs = [
        pl.BlockSpec((chunk, heads * dv), lambda h, c: (c, h)),
        pl.BlockSpec((chunk, heads * dv), lambda h, c: (nc - 1 - c, h)),
    ]
    state = [pltpu.VMEM((dk, dv + LANE), F32), pltpu.VMEM((1, LANE), F32), pltpu.VMEM((chunk, dv + LANE), BF16)]
    return pl.pallas_call(
        functools.partial(_mlstm_kernel, heads=heads, dk=dk, dv=dv),
        grid=(NH_M // heads, nc),
        in_specs=in_specs,
        out_specs=out_specs,
        out_shape=[jax.ShapeDtypeStruct((s, NH_M * dv), BF16)] * 2,
        scratch_shapes=state * (2 * heads),
        compiler_params=_params(("parallel", "arbitrary"), 48),
        name="mlstm_scan",
    )(q, kt, proj, p, pm, pt, q, kt, proj, p, pm, pt)


def _rope_table_kernel(pos_ref, freq_ref, sign_ref, cos_ref, sin_ref):
    angle = pos_ref[...].astype(F32) * freq_ref[...]
    cos_ref[...] = jnp.cos(angle)
    sin_ref[...] = jnp.sin(angle) * sign_ref[...]


def _rope_tables(pos_col, freq_full, sign, *, tb=1024):
    s = pos_col.shape[0]
    dk = freq_full.shape[1]
    return pl.pallas_call(
        _rope_table_kernel,
        grid=(s // tb,),
        in_specs=[
            pl.BlockSpec((tb, 1), lambda i: (i, 0)),
            pl.BlockSpec((1, dk), lambda i: (0, 0)),
            pl.BlockSpec((1, dk), lambda i: (0, 0)),
        ],
        out_specs=[pl.BlockSpec((tb, dk), lambda i: (i, 0))] * 2,
        out_shape=[jax.ShapeDtypeStruct((s, dk), F32)] * 2,
        compiler_params=_params(("parallel",), 32),
        name="rope_tables",
    )(pos_col, freq_full, sign)


def _rope_kernel(q_ref, k_ref, cos_ref, sin_ref, qo_ref, kto_ref, *, q_scale):
    dk = cos_ref.shape[1]
    cos = cos_ref[...]
    sin = sin_ref[...]
    for hh in range(q_ref.shape[1] // dk):
        cols = slice(hh * dk, (hh + 1) * dk)
        q = q_ref[:, cols].astype(F32)
        k = k_ref[:, cols].astype(F32)
        q = (q * cos + pltpu.roll(q, dk // 2, axis=1) * sin) * q_scale
        k = k * cos + pltpu.roll(k, dk // 2, axis=1) * sin
        qo_ref[:, cols] = q.astype(qo_ref.dtype)
        kto_ref[cols, :] = k.T.astype(kto_ref.dtype)


def _rope(proj, cos_full, sin_signed, *, dk, q_col0, k_col0, tb=512):
    s = proj.shape[0]
    width = NH_R * dk
    q_blk0, k_blk0 = q_col0 // width, k_col0 // width
    return pl.pallas_call(
        functools.partial(_rope_kernel, q_scale=dk ** -0.5),
        grid=(s // tb,),
        in_specs=[
            pl.BlockSpec((tb, width), lambda i: (i, q_blk0)),
            pl.BlockSpec((tb, width), lambda i: (i, k_blk0)),
            pl.BlockSpec((tb, dk), lambda i: (i, 0)),
            pl.BlockSpec((tb, dk), lambda i: (i, 0)),
        ],
        out_specs=[
            pl.BlockSpec((tb, width), lambda i: (i, 0)),
            pl.BlockSpec((width, tb), lambda i: (0, i)),
        ],
        out_shape=[jax.ShapeDtypeStruct((s, width), BF16), jax.ShapeDtypeStruct((width, s), BF16)],
        compiler_params=_params(("parallel",), 32),
        name="rope",
    )(proj, proj, cos_full, sin_signed)


def _retention_direction(q_ref, kt_ref, v_ref, o_ref, s_ref, d_ref, qd_ref, kd_ref, cd):
    dv = v_ref.shape[1]
    q = q_ref[...]
    kt = kt_ref[...]
    v = v_ref[...]
    scores = (_dot(q, kt) * d_ref[...]).astype(BF16)
    inter = _dot(q, s_ref[...].astype(BF16))
    o_ref[...] = (_dot(scores, v) + _lanes(qd_ref[...], dv) * inter).astype(o_ref.dtype)
    kt_in = (kt.astype(F32) * kd_ref[...]).astype(BF16)
    s_ref[...] = cd * s_ref[...] + _dot(kt_in, v)


def _retention_kernel(lg_ref, qf, ktf, vf, qb, ktb, vb, of, ob, *scratch, heads, dk, dv):
    chunk = qf.shape[0]
    first = pl.program_id(1) == 0
    for hh in range(heads):
        sf, sb, df, db, qdf, kdf, qdb, kdb = scratch[8 * hh:8 * hh + 8]
        lg = lg_ref[hh]

        @pl.when(first)
        def _():
            sf[...] = jnp.zeros(sf.shape, F32)
            sb[...] = jnp.zeros(sb.shape, F32)
            ri = lax.broadcasted_iota(jnp.int32, (chunk, chunk), 0)
            ci = lax.broadcasted_iota(jnp.int32, (chunk, chunk), 1)
            diff = (ri - ci).astype(F32)
            df[...] = jnp.where(diff >= 0.0, jnp.exp(jnp.where(diff >= 0.0, diff, 0.0) * lg), 0.0)
            db[...] = jnp.where(diff < 0.0, jnp.exp(jnp.where(diff < 0.0, -diff, 0.0) * lg), 0.0)
            row = lax.broadcasted_iota(jnp.int32, (chunk, LANE), 0).astype(F32)
            col = lax.broadcasted_iota(jnp.int32, (1, chunk), 1).astype(F32)
            qdf[...] = jnp.exp((row + 1.0) * lg)
            kdf[...] = jnp.exp((chunk - 1.0 - col) * lg)
            qdb[...] = jnp.exp((chunk - row) * lg)
            kdb[...] = jnp.exp(col * lg)

    for hh in range(heads):
        sf, sb, df, db, qdf, kdf, qdb, kdb = scratch[8 * hh:8 * hh + 8]
        cd = jnp.exp(chunk * lg_ref[hh])
        qk_cols = pl.ds(hh * dk, dk)
        v_cols = pl.ds(hh * dv, dv)
        _retention_direction(qf.at[:, qk_cols], ktf.at[qk_cols, :], vf.at[:, v_cols], of.at[:, v_cols],
                             sf, df, qdf, kdf, cd)
        _retention_direction(qb.at[:, qk_cols], ktb.at[qk_cols, :], vb.at[:, v_cols], ob.at[:, v_cols],
                             sb, db, qdb, kdb, cd)


def _retention_scan(q, kt, proj, log_gamma, *, dk, dv, v_col0, chunk, heads=HEADS_PER_STEP_R):
    s = q.shape[0]
    nc = s // chunk
    v_blk0 = v_col0 // (heads * dv)

    def fwd(c):
        return c

    def bwd(c):
        return nc - 1 - c

    in_specs = [pl.BlockSpec((heads, 1, 1), lambda h, c: (h, 0, 0))]
    for cmap in (fwd, bwd):
        in_specs += [
            pl.BlockSpec((chunk, heads * dk), lambda h, c, cmap=cmap: (cmap(c), h)),
            pl.BlockSpec((heads * dk, chunk), lambda h, c, cmap=cmap: (h, cmap(c))),
            pl.BlockSpec((chunk, heads * dv), lambda h, c, cmap=cmap: (cmap(c), v_blk0 + h)),
        ]
    out_specs = [
        pl.BlockSpec((chunk, heads * dv), lambda h, c: (c, h)),
        pl.BlockSpec((chunk, heads * dv), lambda h, c: (nc - 1 - c, h)),
    ]
    per_head = [
        pltpu.VMEM((dk, dv), F32), pltpu.VMEM((dk, dv), F32),
        pltpu.VMEM((chunk, chunk), F32), pltpu.VMEM((chunk, chunk), F32),
        pltpu.VMEM((chunk, LANE), F32), pltpu.VMEM((1, chunk), F32),
        pltpu.VMEM((chunk, LANE), F32), pltpu.VMEM((1, chunk), F32),
    ]
    return pl.pallas_call(
        functools.partial(_retention_kernel, heads=heads, dk=dk, dv=dv),
        grid=(NH_R // heads, nc),
        in_specs=in_specs,
        out_specs=out_specs,
        out_shape=[jax.ShapeDtypeStruct((s, NH_R * dv), BF16)] * 2,
        scratch_shapes=per_head * heads,
        compiler_params=_params(("parallel", "arbitrary"), 40),
        name="retention_scan",
    )(log_gamma, q, kt, proj, q, kt, proj)


def _head_norm_gate(h, gain_ref, z_ref, u_ref, n_heads):
    width = h.shape[1] // n_heads
    for hh in range(n_heads):
        cols = slice(hh * width, (hh + 1) * width)
        seg = h[:, cols]
        y = seg * lax.rsqrt(jnp.mean(seg * seg, axis=-1, keepdims=True) + EPS) * gain_ref[:, cols]
        z = z_ref[:, cols].astype(F32)
        u_ref[:, cols] = (y * _silu(z)).astype(BF16)


def _merge_kernel(hmf, hmb, om, zm, hrf, hrb, zr, mw, rw, wpm, wpr, gm, gr, bmix, y_ref, um, ur, *, tn):
    d = y_ref.shape[1]
    h_m = _sigmoid(om[...].astype(F32)) * (hmf[...].astype(F32) + hmb[...].astype(F32))
    _head_norm_gate(h_m, mw, zm, um, NH_M)
    _head_norm_gate(hrf[...].astype(F32) + hrb[...].astype(F32), rw, zr, ur, NH_R)
    for nb in range(d // tn):
        cols = slice(nb * tn, (nb + 1) * tn)
        cols_r = slice(d + nb * tn, d + (nb + 1) * tn)
        mix_m = _sigmoid(gm[:, cols].astype(F32) + bmix[:, cols])
        mix_r = _sigmoid(gr[:, cols].astype(F32) + bmix[:, cols_r])
        y = mix_m * _dot(um[...], wpm[:, cols]) + mix_r * _dot(ur[...], wpr[:, cols])
        y_ref[:, cols] = y.astype(y_ref.dtype)


def _merge(hmf, hmb, hrf, hrb, proj, m_norm_w, r_norm_w, w_proj_m, w_proj_r, b_mix, *,
           om_col0, zm_col0, zr_col0, gm_col0, gr_col0, tm=256, tn=512):
    s, d = hmf.shape
    wide = lambda col0: pl.BlockSpec((tm, d), lambda i: (i, col0 // d))
    row_full = pl.BlockSpec((tm, d), lambda i: (i, 0))
    const = lambda shape: pl.BlockSpec(shape, lambda i: (0, 0), pipeline_mode=pl.Buffered(1))
    return pl.pallas_call(
        functools.partial(_merge_kernel, tn=tn),
        grid=(s // tm,),
        in_specs=[
            row_full, row_full, wide(om_col0), wide(zm_col0),
            row_full, row_full, wide(zr_col0),
            const((1, d)), const((1, d)), const((d, d)), const((d, d)),
            wide(gm_col0), wide(gr_col0), const((1, 2 * d)),
        ],
        out_specs=pl.BlockSpec((tm, d), lambda i: (i, 0)),
        out_shape=jax.ShapeDtypeStruct((s, d), BF16),
        scratch_shapes=[pltpu.VMEM((tm, d), BF16), pltpu.VMEM((tm, d), BF16)],
        compiler_params=_params(("parallel",), 48),
        name="merge",
    )(hmf, hmb, proj, proj, hrf, hrb, proj, m_norm_w, r_norm_w, w_proj_m, w_proj_r, proj, proj, b_mix)


def _outproj_kernel(x_ref, y_ref, w_ref, fw_ref, o_ref, *, final_norm):
    out = x_ref[...] + _dot(y_ref[...], w_ref[...])
    if final_norm:
        ms = jnp.mean(out * out, axis=-1, keepdims=True)
        out = (out * lax.rsqrt(ms + EPS)) * fw_ref[...]
    o_ref[...] = out


def _outproj(x2, y, w_out, final_w, *, final_norm, tm=512):
    s, d = x2.shape
    return pl.pallas_call(
        functools.partial(_outproj_kernel, final_norm=final_norm),
        grid=(s // tm,),
        in_specs=[
            pl.BlockSpec((tm, d), lambda i: (i, 0)),
            pl.BlockSpec((tm, d), lambda i: (i, 0)),
            pl.BlockSpec((d, d), lambda i: (0, 0)),
            pl.BlockSpec((1, d), lambda i: (0, 0)),
        ],
        out_specs=pl.BlockSpec((tm, d), lambda i: (i, 0)),
        out_shape=jax.ShapeDtypeStruct((s, d), F32),
        compiler_params=_params(("parallel",), 48),
        name="outproj",
    )(x2, y, w_out, final_w)


def kernel(x, positions, norm_w, w_in, b_mgate, conv_w, conv_b, m_norm_w, r_norm_w,
           w_proj_m, w_proj_r, b_mix, w_out, final_norm_w):
    batch, s, d = x.shape
    assert batch == 1, "kernel is written for BATCH == 1"
    depth = norm_w.shape[0]
    dqk_m, dv_m = d // 8, d // 4
    dqk_r, dv_r = d // 16, d // 8
    wm_qk, wm_v = NH_M * dqk_m, NH_M * dv_m
    wr_qk, wr_v = NH_R * dqk_r, NH_R * dv_r
    n_gate = N_GATE_TYPES * NH_M
    col_k_m = wm_qk
    col_v_m = 2 * wm_qk
    col_z_m = col_v_m + wm_v
    col_o_m = col_z_m + wm_v
    col_gate = col_o_m + wm_v
    col_q_r = col_gate
    col_k_r = col_q_r + wr_qk
    col_v_r = col_k_r + wr_qk
    col_z_r = col_v_r + wr_v
    col_gate_m = col_z_r + wr_v
    col_gate_r = col_gate_m + d

    x2 = x.reshape(s, d)
    log_gamma = jnp.log1p(-jnp.power(2.0, -5.0 - jnp.arange(NH_R, dtype=F32))).reshape(NH_R, 1, 1)
    inv_freq = jnp.power(ROPE_BASE, -jnp.arange(dqk_r // 2, dtype=F32) / (dqk_r // 2))
    freq_full = jnp.concatenate([inv_freq, inv_freq]).reshape(1, dqk_r)
    sign = jnp.concatenate([-jnp.ones((dqk_r // 2,), F32), jnp.ones((dqk_r // 2,), F32)]).reshape(1, dqk_r)
    cos_full, sin_signed = _rope_tables(positions.reshape(s, 1), freq_full, sign)

    for layer in range(depth):
        w_l = w_in[layer]
        w_main = jnp.concatenate([w_l[:, :col_gate], w_l[:, col_gate + n_gate:]], axis=1).astype(BF16)
        w_gate = jnp.pad(w_l[:, col_gate:col_gate + n_gate], ((0, 0), (0, LANE - n_gate))).astype(BF16)
        b_gate = jnp.pad(b_mgate[layer], (0, LANE - n_gate)).reshape(1, LANE)

        proj, gates = _inproj(x2, norm_w[layer].reshape(1, d), w_main, w_gate, b_gate)

        p, pm, pt = _gate_prep(gates, CHUNK_M)

        cw, cb = conv_w[layer], conv_b[layer].reshape(1, 2 * wm_qk)
        q_m = _conv_silu(proj, cw, cb, col0=0, width=wm_qk, scale=dqk_m ** -0.5, transpose_out=False)
        kt_m = _conv_silu(proj, cw, cb, col0=col_k_m, width=wm_qk, scale=1.0, transpose_out=True)
        hmf, hmb = _mlstm_scan(q_m, kt_m, proj, p, pm, pt, dk=dqk_m, dv=dv_m, v_col0=col_v_m, chunk=CHUNK_M)

        q_r, kt_r = _rope(proj, cos_full, sin_signed, dk=dqk_r, q_col0=col_q_r, k_col0=col_k_r)
        hrf, hrb = _retention_scan(q_r, kt_r, proj, log_gamma, dk=dqk_r, dv=dv_r, v_col0=col_v_r, chunk=CHUNK_R)

        y = _merge(hmf, hmb, hrf, hrb, proj, m_norm_w[layer].reshape(1, wm_v), r_norm_w[layer].reshape(1, wr_v),
                   w_proj_m[layer].astype(BF16), w_proj_r[layer].astype(BF16), b_mix[layer].reshape(1, 2 * d),
                   om_col0=col_o_m, zm_col0=col_z_m, zr_col0=col_z_r, gm_col0=col_gate_m, gr_col0=col_gate_r)
        x2 = _outproj(x2, y, w_out[layer].astype(BF16), final_norm_w.reshape(1, d),
                      final_norm=(layer == depth - 1))
    return x2.reshape(batch, s, d)
```

```python
import functools

import jax
import jax.numpy as jnp
from jax import lax
from jax.experimental import pallas as pl
from jax.experimental.pallas import tpu as pltpu

NH_M = 4
NH_R = 8
N_GATE_TYPES = 4
CONV_W = 5
ROPE_BASE = 10000.0
EPS = 1e-6

LANE = 128
BF16_ROWS = 16
MIB = 1024 * 1024

CHUNK_M = 256
CHUNK_R = 256
HEADS_PER_STEP_R = 4

F32 = jnp.float32
BF16 = jnp.bfloat16


def _params(semantics, vmem_mib):
    return pltpu.CompilerParams(dimension_semantics=semantics, vmem_limit_bytes=vmem_mib * MIB)


def _sigmoid(x):
    return 0.5 * jnp.tanh(0.5 * x) + 0.5


def _silu(x):
    half = 0.5 * x
    return half + half * jnp.tanh(half)


def _lanes(x, width):
    return x if width == LANE else jnp.concatenate([x] * (width // LANE), axis=1)


def _dot(a, b):
    return jnp.dot(a, b, preferred_element_type=F32)


def _cast_kernel(x_ref, o_ref):
    o_ref[...] = x_ref[...].astype(o_ref.dtype)


def _cast_bf16(w, *, tr=512):
    depth, rows, cols = w.shape
    spec = pl.BlockSpec((None, tr, cols), lambda l, i: (l, i, 0))
    return pl.pallas_call(
        _cast_kernel,
        grid=(depth, rows // tr),
        in_specs=[spec],
        out_specs=spec,
        out_shape=jax.ShapeDtypeStruct(w.shape, BF16),
        compiler_params=_params(("parallel", "parallel"), 32),
        name="cast_bf16",
    )(w)


def _drop_gate_cols_kernel(a_ref, b_ref, o_ref, *, n_plain, shift):
    tr, tn = a_ref.shape

    @pl.when(pl.program_id(2) < n_plain)
    def _():
        o_ref[...] = a_ref[...].astype(o_ref.dtype)

    @pl.when(pl.program_id(2) >= n_plain)
    def _():
        a_sh = pltpu.roll(a_ref[...], tn - shift, axis=1)
        b_sh = pltpu.roll(b_ref[...], LANE - shift, axis=1)
        lane = lax.broadcasted_iota(jnp.int32, (tr, LANE), 1)
        o_ref[:, :tn - LANE] = a_sh[:, :tn - LANE].astype(o_ref.dtype)
        o_ref[:, tn - LANE:] = jnp.where(lane < LANE - shift, a_sh[:, tn - LANE:], b_sh).astype(o_ref.dtype)


def _drop_gate_cols_bf16(w_in, gate_col0, n_gate, *, tr=512, tn=1024):
    depth, rows, cols = w_in.shape
    n_out = cols - n_gate
    lane_blocks = tn // LANE
    return pl.pallas_call(
        functools.partial(_drop_gate_cols_kernel, n_plain=gate_col0 // tn, shift=n_gate),
        grid=(depth, rows // tr, n_out // tn),
        in_specs=[
            pl.BlockSpec((None, tr, tn), lambda l, i, j: (l, i, j)),
            pl.BlockSpec((None, tr, LANE), lambda l, i, j: (l, i, (j + 1) * lane_blocks)),
        ],
        out_specs=pl.BlockSpec((None, tr, tn), lambda l, i, j: (l, i, j)),
        out_shape=jax.ShapeDtypeStruct((depth, rows, n_out), BF16),
        compiler_params=_params(("parallel", "parallel", "parallel"), 32),
        name="drop_gate_cols",
    )(w_in, w_in)


def _inproj_kernel(x_ref, nw_ref, w_ref, wg_ref, bg_ref, o_ref, g_ref, h_scr, *, sub_rows):
    tm = x_ref.shape[0]

    @pl.when(pl.program_id(1) == 0)
    def _():
        def body(r, carry):
            rs = pl.ds(pl.multiple_of(r * sub_rows, sub_rows), sub_rows)
            x = x_ref[rs, :]
            ms = jnp.mean(x * x, axis=-1, keepdims=True)
            h = (x * lax.rsqrt(ms + EPS)) * nw_ref[...]
            h_scr[rs, :] = h.astype(BF16)
            return carry

        lax.fori_loop(0, tm // sub_rows, body, 0)
        g_ref[...] = _dot(h_scr[...], wg_ref[...]) + bg_ref[...]

    o_ref[...] = _dot(h_scr[...], w_ref[...]).astype(o_ref.dtype)


def _inproj(x2, norm_w, w_main, layer, w_gate, b_gate, *, tm=1024, tn=2048):
    s, d = x2.shape
    n = w_main.shape[2]
    return pl.pallas_call(
        functools.partial(_inproj_kernel, sub_rows=256),
        grid=(s // tm, n // tn),
        in_specs=[
            pl.BlockSpec((tm, d), lambda i, j: (i, 0)),
            pl.BlockSpec((1, d), lambda i, j: (0, 0)),
            pl.BlockSpec((None, d, tn), lambda i, j: (layer, 0, j)),
            pl.BlockSpec((d, LANE), lambda i, j: (0, 0)),
            pl.BlockSpec((1, LANE), lambda i, j: (0, 0)),
        ],
        out_specs=[
            pl.BlockSpec((tm, tn), lambda i, j: (i, j)),
            pl.BlockSpec((tm, LANE), lambda i, j: (i, 0)),
        ],
        out_shape=[jax.ShapeDtypeStruct((s, n), BF16), jax.ShapeDtypeStruct((s, LANE), F32)],
        scratch_shapes=[pltpu.VMEM((tm, d), BF16)],
        compiler_params=_params(("parallel", "arbitrary"), 56),
        name="inproj",
    )(x2, norm_w, w_main, w_gate, b_gate)


def _gate_prep_kernel(g_ref, p_ref, pm_ref, pt_ref):
    g = g_ref[...]
    chunk = g.shape[0]
    log_f = jnp.minimum(g, 0.0) - jnp.log1p(jnp.exp(-jnp.abs(g)))
    ri = lax.broadcasted_iota(jnp.int32, (chunk, chunk), 0)
    ci = lax.broadcasted_iota(jnp.int32, (chunk, chunk), 1)
    tril = (ci <= ri).astype(F32)
    triu = (ci >= ri).astype(F32)
    cum_prefix = jnp.dot(tril, log_f, precision=lax.Precision.HIGHEST, preferred_element_type=F32)
    cum_suffix = jnp.dot(triu, log_f, precision=lax.Precision.HIGHEST, preferred_element_type=F32)
    lane = lax.broadcasted_iota(jnp.int32, g.shape, 1)
    row = lax.broadcasted_iota(jnp.int32, g.shape, 0)
    cum = jnp.where(lane < 2 * NH_M, cum_prefix, cum_suffix)
    b = g - pltpu.roll(cum, LANE - NH_M, axis=1)
    p = jnp.where((lane % (2 * NH_M)) < NH_M, b, cum)
    p_ref[...] = p
    pt_ref[...] = p.T[:pt_ref.shape[0], :]

    pm_f = b
    pm_b = b
    step = 1
    while step < chunk:
        pm_f = jnp.maximum(pm_f, jnp.where(row >= step, pltpu.roll(pm_f, step, axis=0), -jnp.inf))
        pm_b = jnp.maximum(pm_b, jnp.where(row < chunk - step, pltpu.roll(pm_b, chunk - step, axis=0), -jnp.inf))
        step *= 2
    pm_ref[...] = jnp.where(lane < 2 * NH_M, pm_f, pm_b)


def _gate_prep(g, chunk):
    s = g.shape[0]
    n_rows = N_GATE_TYPES * NH_M
    spec = pl.BlockSpec((chunk, LANE), lambda c: (c, 0))
    return pl.pallas_call(
        _gate_prep_kernel,
        grid=(s // chunk,),
        in_specs=[spec],
        out_specs=[spec, spec, pl.BlockSpec((n_rows, chunk), lambda c: (0, c))],
        out_shape=[jax.ShapeDtypeStruct((s, LANE), F32)] * 2 + [jax.ShapeDtypeStruct((n_rows, s), F32)],
        compiler_params=_params(("parallel",), 32),
        name="gate_prep",
    )(g)


def _conv_kernel(xp_ref, x_ref, xn_ref, w_ref, b_ref, o_ref, xe_ref, *, scale, transpose_out):
    i = pl.program_id(0)
    tb = x_ref.shape[0]
    halo = CONV_W // 2
    xe_ref[0:BF16_ROWS, :] = jnp.where(i > 0, xp_ref[...].astype(F32), 0.0)
    xe_ref[BF16_ROWS:BF16_ROWS + tb, :] = x_ref[...].astype(F32)
    xe_ref[BF16_ROWS + tb:2 * BF16_ROWS + tb, :] = jnp.where(
        i < pl.num_programs(0) - 1, xn_ref[...].astype(F32), 0.0)
    acc = b_ref[...] + w_ref[0:1, :] * xe_ref[pl.ds(BF16_ROWS - halo, tb), :]
    for k in range(1, CONV_W):
        acc = acc + w_ref[k:k + 1, :] * xe_ref[pl.ds(BF16_ROWS - halo + k, tb), :]
    y = _silu(acc)
    if scale != 1.0:
        y = y * scale
    o_ref[...] = (y.T if transpose_out else y).astype(o_ref.dtype)


def _conv_silu(proj, conv_w, conv_b, *, col0, width, scale, transpose_out, tb=512, tc=512):
    s = proj.shape[0]
    rb = tb // BF16_ROWS
    n_rb = s // BF16_ROWS
    cb0 = col0 // tc
    if transpose_out:
        out_spec = pl.BlockSpec((tc, tb), lambda i, j: (j, i))
        out_shape = jax.ShapeDtypeStruct((width, s), BF16)
    else:
        out_spec = pl.BlockSpec((tb, tc), lambda i, j: (i, j))
        out_shape = jax.ShapeDtypeStruct((s, width), BF16)
    return pl.pallas_call(
        functools.partial(_conv_kernel, scale=scale, transpose_out=transpose_out),
        grid=(s // tb, width // tc),
        in_specs=[
            pl.BlockSpec((BF16_ROWS, tc), lambda i, j: (jnp.maximum(i * rb - 1, 0), cb0 + j)),
            pl.BlockSpec((tb, tc), lambda i, j: (i, cb0 + j)),
            pl.BlockSpec((BF16_ROWS, tc), lambda i, j: (jnp.minimum((i + 1) * rb, n_rb - 1), cb0 + j)),
            pl.BlockSpec((CONV_W, tc), lambda i, j: (0, cb0 + j)),
            pl.BlockSpec((1, tc), lambda i, j: (0, cb0 + j)),
        ],
        out_specs=out_spec,
        out_shape=out_shape,
        scratch_shapes=[pltpu.VMEM((tb + 2 * BF16_ROWS, tc), F32)],
        compiler_params=_params(("parallel", "parallel"), 32),
        name="conv_silu_t" if transpose_out else "conv_silu",
    )(proj, proj, proj, conv_w, conv_b)


def _mlstm_direction(q_ref, kt_ref, v_ref, p_ref, pm_ref, pt_ref, o_ref, c_ref, m_ref, va_ref, *, head, reverse):
    chunk, dv = v_ref.shape
    q = q_ref[...]
    kt = kt_ref[...]
    va_ref[:, 0:dv] = v_ref[...]
    va = va_ref[...]
    lane_b = (2 * NH_M if reverse else 0) + head
    lane_cum = lane_b + NH_M
    cum = jnp.broadcast_to(p_ref[:, lane_cum:lane_cum + 1], (chunk, LANE))
    pmax = jnp.broadcast_to(pm_ref[:, lane_b:lane_b + 1], (chunk, LANE))
    b_row = pt_ref[lane_b:lane_b + 1, :]
    last = 0 if reverse else chunk - 1
    total = cum[last:last + 1, :]
    b_max = pmax[last:last + 1, :]
    m_prev = m_ref[...]

    m_loc = jnp.maximum(pmax, m_prev)
    ri = lax.broadcasted_iota(jnp.int32, (chunk, chunk), 0)
    ci = lax.broadcasted_iota(jnp.int32, (chunk, chunk), 1)
    mask = (ci >= ri) if reverse else (ci <= ri)
    expo = jnp.where(mask, b_row - _lanes(m_loc, chunk), -jnp.inf)
    w = (jnp.exp(expo) * _dot(q, kt)).astype(BF16)
    a = jnp.exp(m_prev - m_loc)
    q_c = _dot(q, c_ref[...].astype(BF16))
    num_den = _dot(w, va) + _lanes(a, dv + LANE) * q_c
    den = num_den[:, dv:dv + LANE]
    inv = 1.0 / jnp.maximum(jnp.abs(den), jnp.exp(-(cum + m_loc)))
    for jb in range(dv // LANE):
        cols = slice(jb * LANE, (jb + 1) * LANE)
        o_ref[:, cols] = (num_den[:, cols] * inv).astype(o_ref.dtype)

    m_in = jnp.maximum(m_prev, b_max)
    s_in = jnp.exp(b_row - _lanes(m_in, chunk))
    decay = jnp.exp(m_prev - m_in)
    kt_in = (kt.astype(F32) * s_in).astype(BF16)
    c_ref[...] = _lanes(decay, dv + LANE) * c_ref[...] + _dot(kt_in, va)
    m_ref[...] = total + m_in


def _mlstm_kernel(qf, ktf, vf, pf, pmf, ptf, qb, ktb, vb, pb, pmb, ptb, of, ob, *scratch, heads, dk, dv):
    fwd_state, bwd_state = scratch[:3 * heads], scratch[3 * heads:]

    @pl.when(pl.program_id(1) == 0)
    def _():
        for hh in range(2 * heads):
            c_ref, m_ref, va_ref = scratch[3 * hh:3 * hh + 3]
            c_ref[...] = jnp.zeros(c_ref.shape, c_ref.dtype)
            m_ref[...] = jnp.zeros(m_ref.shape, m_ref.dtype)
            va_ref[:, dv:dv + LANE] = jnp.ones((va_ref.shape[0], LANE), va_ref.dtype)

    for hh in range(heads):
        qk_cols = pl.ds(hh * dk, dk)
        v_cols = pl.ds(hh * dv, dv)
        _mlstm_direction(qf.at[:, qk_cols], ktf.at[qk_cols, :], vf.at[:, v_cols], pf, pmf, ptf,
                         of.at[:, v_cols], *fwd_state[3 * hh:3 * hh + 3], head=hh, reverse=False)
        _mlstm_direction(qb.at[:, qk_cols], ktb.at[qk_cols, :], vb.at[:, v_cols], pb, pmb, ptb,
                         ob.at[:, v_cols], *bwd_state[3 * hh:3 * hh + 3], head=hh, reverse=True)


def _mlstm_scan(q, kt, proj, p, pm, pt, *, dk, dv, v_col0, chunk):
    heads = NH_M
    s = q.shape[0]
    nc = s // chunk
    v_blk0 = v_col0 // (heads * dv)
    n_row = pt.shape[0]

    def fwd(c):
        return c

    def bwd(c):
        return nc - 1 - c

    in_specs = []
    for cmap in (fwd, bwd):
        in_specs += [
            pl.BlockSpec((chunk, heads * dk), lambda h, c, cmap=cmap: (cmap(c), h)),
            pl.BlockSpec((heads * dk, chunk), lambda h, c, cmap=cmap: (h, cmap(c))),
            pl.BlockSpec((chunk, heads * dv), lambda h, c, cmap=cmap: (cmap(c), v_blk0 + h)),
            pl.BlockSpec((chunk, LANE), lambda h, c, cmap=cmap: (cmap(c), 0)),
            pl.BlockSpec((chunk, LANE), lambda h, c, cmap=cmap: (cmap(c), 0)),
            pl.BlockSpec((n_row, chunk), lambda h, c, cmap=cmap: (0, cmap(c))),
        ]
    out_specs = [
        pl.BlockSpec((chunk, heads * dv), lambda h, c: (c, h)),
        pl.BlockSpec((chunk, heads * dv), lambda h, c: (nc - 1 - c, h)),
    ]
    state = [pltpu.VMEM((dk, dv + LANE), F32), pltpu.VMEM((1, LANE), F32), pltpu.VMEM((chunk, dv + LANE), BF16)]
    return pl.pallas_call(
        functools.partial(_mlstm_kernel, heads=heads, dk=dk, dv=dv),
        grid=(NH_M // heads, nc),
        in_specs=in_specs,
        out_specs=out_specs,
        out_shape=[jax.ShapeDtypeStruct((s, NH_M * dv), BF16)] * 2,
        scratch_shapes=state * (2 * heads),
        compiler_params=_params(("parallel", "arbitrary"), 48),
        name="mlstm_scan",
    )(q, kt, proj, p, pm, pt, q, kt, proj, p, pm, pt)


def _rope_table_kernel(pos_ref, freq_ref, sign_ref, cos_ref, sin_ref):
    angle = pos_ref[...].astype(F32) * freq_ref[...]
    cos_ref[...] = jnp.cos(angle)
    sin_ref[...] = jnp.sin(angle) * sign_ref[...]


def _rope_tables(pos_col, freq_full, sign, *, tb=1024):
    s = pos_col.shape[0]
    dk = freq_full.shape[1]
    return pl.pallas_call(
        _rope_table_kernel,
        grid=(s // tb,),
        in_specs=[
            pl.BlockSpec((tb, 1), lambda i: (i, 0)),
            pl.BlockSpec((1, dk), lambda i: (0, 0)),
            pl.BlockSpec((1, dk), lambda i: (0, 0)),
        ],
        out_specs=[pl.BlockSpec((tb, dk), lambda i: (i, 0))] * 2,
        out_shape=[jax.ShapeDtypeStruct((s, dk), F32)] * 2,
        compiler_params=_params(("parallel",), 32),
        name="rope_tables",
    )(pos_col, freq_full, sign)


def _rope_kernel(q_ref, k_ref, cos_ref, sin_ref, qo_ref, kto_ref, *, q_scale):
    dk = cos_ref.shape[1]
    cos = cos_ref[...]
    sin = sin_ref[...]
    for hh in range(q_ref.shape[1] // dk):
        cols = slice(hh * dk, (hh + 1) * dk)
        q = q_ref[:, cols].astype(F32)
        k = k_ref[:, cols].astype(F32)
        q = (q * cos + pltpu.roll(q, dk // 2, axis=1) * sin) * q_scale
        k = k * cos + pltpu.roll(k, dk // 2, axis=1) * sin
        qo_ref[:, cols] = q.astype(qo_ref.dtype)
        kto_ref[cols, :] = k.T.astype(kto_ref.dtype)


def _rope(proj, cos_full, sin_signed, *, dk, q_col0, k_col0, tb=512):
    s = proj.shape[0]
    width = NH_R * dk
    q_blk0, k_blk0 = q_col0 // width, k_col0 // width
    return pl.pallas_call(
        functools.partial(_rope_kernel, q_scale=dk ** -0.5),
        grid=(s // tb,),
        in_specs=[
            pl.BlockSpec((tb, width), lambda i: (i, q_blk0)),
            pl.BlockSpec((tb, width), lambda i: (i, k_blk0)),
            pl.BlockSpec((tb, dk), lambda i: (i, 0)),
            pl.BlockSpec((tb, dk), lambda i: (i, 0)),
        ],
        out_specs=[
            pl.BlockSpec((tb, width), lambda i: (i, 0)),
            pl.BlockSpec((width, tb), lambda i: (0, i)),
        ],
        out_shape=[jax.ShapeDtypeStruct((s, width), BF16), jax.ShapeDtypeStruct((width, s), BF16)],
        compiler_params=_params(("parallel",), 32),
        name="rope",
    )(proj, proj, cos_full, sin_signed)


def _retention_direction(q_ref, kt_ref, v_ref, o_ref, s_ref, d_ref, qd_ref, kd_ref, cd):
    dv = v_ref.shape[1]
    q = q_ref[...]
    kt = kt_ref[...]
    v = v_ref[...]
    scores = (_dot(q, kt) * d_ref[...]).astype(BF16)
    inter = _dot(q, s_ref[...].astype(BF16))
    o_ref[...] = (_dot(scores, v) + _lanes(qd_ref[...], dv) * inter).astype(o_ref.dtype)
    kt_in = (kt.astype(F32) * kd_ref[...]).astype(BF16)
    s_ref[...] = cd * s_ref[...] + _dot(kt_in, v)


def _retention_kernel(lg_ref, qf, ktf, vf, qb, ktb, vb, of, ob, *scratch, heads, dk, dv):
    chunk = qf.shape[0]
    first = pl.program_id(1) == 0
    for hh in range(heads):
        sf, sb, df, db, qdf, kdf, qdb, kdb = scratch[8 * hh:8 * hh + 8]
        lg = lg_ref[hh]

        @pl.when(first)
        def _():
            sf[...] = jnp.zeros(sf.shape, F32)
            sb[...] = jnp.zeros(sb.shape, F32)
            ri = lax.broadcasted_iota(jnp.int32, (chunk, chunk), 0)
            ci = lax.broadcasted_iota(jnp.int32, (chunk, chunk), 1)
            diff = (ri - ci).astype(F32)
            df[...] = jnp.where(diff >= 0.0, jnp.exp(jnp.where(diff >= 0.0, diff, 0.0) * lg), 0.0)
            db[...] = jnp.where(diff < 0.0, jnp.exp(jnp.where(diff < 0.0, -diff, 0.0) * lg), 0.0)
            row = lax.broadcasted_iota(jnp.int32, (chunk, LANE), 0).astype(F32)
            col = lax.broadcasted_iota(jnp.int32, (1, chunk), 1).astype(F32)
            qdf[...] = jnp.exp((row + 1.0) * lg)
            kdf[...] = jnp.exp((chunk - 1.0 - col) * lg)
            qdb[...] = jnp.exp((chunk - row) * lg)
            kdb[...] = jnp.exp(col * lg)

    for hh in range(heads):
        sf, sb, df, db, qdf, kdf, qdb, kdb = scratch[8 * hh:8 * hh + 8]
        cd = jnp.exp(chunk * lg_ref[hh])
        qk_cols = pl.ds(hh * dk, dk)
        v_cols = pl.ds(hh * dv, dv)
        _retention_direction(qf.at[:, qk_cols], ktf.at[qk_cols, :], vf.at[:, v_cols], of.at[:, v_cols],
                             sf, df, qdf, kdf, cd)
        _retention_direction(qb.at[:, qk_cols], ktb.at[qk_cols, :], vb.at[:, v_cols], ob.at[:, v_cols],
                             sb, db, qdb, kdb, cd)


def _retention_scan(q, kt, proj, log_gamma, *, dk, dv, v_col0, chunk, heads=HEADS_PER_STEP_R):
    s = q.shape[0]
    nc = s // chunk
    v_blk0 = v_col0 // (heads * dv)

    def fwd(c):
        return c

    def bwd(c):
        return nc - 1 - c

    in_specs = [pl.BlockSpec((heads, 1, 1), lambda h, c: (h, 0, 0))]
    for cmap in (fwd, bwd):
        in_specs += [
            pl.BlockSpec((chunk, heads * dk), lambda h, c, cmap=cmap: (cmap(c), h)),
            pl.BlockSpec((heads * dk, chunk), lambda h, c, cmap=cmap: (h, cmap(c))),
            pl.BlockSpec((chunk, heads * dv), lambda h, c, cmap=cmap: (cmap(c), v_blk0 + h)),
        ]
    out_specs = [
        pl.BlockSpec((chunk, heads * dv), lambda h, c: (c, h)),
        pl.BlockSpec((chunk, heads * dv), lambda h, c: (nc - 1 - c, h)),
    ]
    per_head = [
        pltpu.VMEM((dk, dv), F32), pltpu.VMEM((dk, dv), F32),
        pltpu.VMEM((chunk, chunk), F32), pltpu.VMEM((chunk, chunk), F32),
        pltpu.VMEM((chunk, LANE), F32), pltpu.VMEM((1, chunk), F32),
        pltpu.VMEM((chunk, LANE), F32), pltpu.VMEM((1, chunk), F32),
    ]
    return pl.pallas_call(
        functools.partial(_retention_kernel, heads=heads, dk=dk, dv=dv),
        grid=(NH_R // heads, nc),
        in_specs=in_specs,
        out_specs=out_specs,
        out_shape=[jax.ShapeDtypeStruct((s, NH_R * dv), BF16)] * 2,
        scratch_shapes=per_head * heads,
        compiler_params=_params(("parallel", "arbitrary"), 40),
        name="retention_scan",
    )(log_gamma, q, kt, proj, q, kt, proj)


def _head_norm_gate(h, gain_ref, z_ref, u_ref, n_heads):
    width = h.shape[1] // n_heads
    for hh in range(n_heads):
        cols = slice(hh * width, (hh + 1) * width)
        seg = h[:, cols]
        y = seg * lax.rsqrt(jnp.mean(seg * seg, axis=-1, keepdims=True) + EPS) * gain_ref[:, cols]
        z = z_ref[:, cols].astype(F32)
        u_ref[:, cols] = (y * _silu(z)).astype(BF16)


def _merge_kernel(hmf, hmb, om, zm, hrf, hrb, zr, mw, rw, wpm, wpr, gm, gr, bmix, y_ref, um, ur, *, tn):
    d = y_ref.shape[1]
    h_m = _sigmoid(om[...].astype(F32)) * (hmf[...].astype(F32) + hmb[...].astype(F32))
    _head_norm_gate(h_m, mw, zm, um, NH_M)
    _head_norm_gate(hrf[...].astype(F32) + hrb[...].astype(F32), rw, zr, ur, NH_R)
    for nb in range(d // tn):
        cols = slice(nb * tn, (nb + 1) * tn)
        cols_r = slice(d + nb * tn, d + (nb + 1) * tn)
        mix_m = _sigmoid(gm[:, cols].astype(F32) + bmix[:, cols])
        mix_r = _sigmoid(gr[:, cols].astype(F32) + bmix[:, cols_r])
        y = mix_m * _dot(um[...], wpm[:, cols]) + mix_r * _dot(ur[...], wpr[:, cols])
        y_ref[:, cols] = y.astype(y_ref.dtype)


def _merge(hmf, hmb, hrf, hrb, proj, m_norm_w, r_norm_w, w_proj_m, w_proj_r, layer, b_mix, *,
           om_col0, zm_col0, zr_col0, gm_col0, gr_col0, tm=256, tn=512):
    s, d = hmf.shape
    wide = lambda col0: pl.BlockSpec((tm, d), lambda i: (i, col0 // d))
    row_full = pl.BlockSpec((tm, d), lambda i: (i, 0))
    const = lambda shape: pl.BlockSpec(shape, lambda i: (0, 0), pipeline_mode=pl.Buffered(1))
    weight = pl.BlockSpec((None, d, d), lambda i: (layer, 0, 0), pipeline_mode=pl.Buffered(1))
    return pl.pallas_call(
        functools.partial(_merge_kernel, tn=tn),
        grid=(s // tm,),
        in_specs=[
            row_full, row_full, wide(om_col0), wide(zm_col0),
            row_full, row_full, wide(zr_col0),
            const((1, d)), const((1, d)), weight, weight,
            wide(gm_col0), wide(gr_col0), const((1, 2 * d)),
        ],
        out_specs=pl.BlockSpec((tm, d), lambda i: (i, 0)),
        out_shape=jax.ShapeDtypeStruct((s, d), BF16),
        scratch_shapes=[pltpu.VMEM((tm, d), BF16), pltpu.VMEM((tm, d), BF16)],
        compiler_params=_params(("parallel",), 48),
        name="merge",
    )(hmf, hmb, proj, proj, hrf, hrb, proj, m_norm_w, r_norm_w, w_proj_m, w_proj_r, proj, proj, b_mix)


def _outproj_kernel(x_ref, y_ref, w_ref, fw_ref, o_ref, *, final_norm):
    out = x_ref[...] + _dot(y_ref[...], w_ref[...])
    if final_norm:
        ms = jnp.mean(out * out, axis=-1, keepdims=True)
        out = (out * lax.rsqrt(ms + EPS)) * fw_ref[...]
    o_ref[...] = out


def _outproj(x2, y, w_out, layer, final_w, *, final_norm, tm=512):
    s, d = x2.shape
    return pl.pallas_call(
        functools.partial(_outproj_kernel, final_norm=final_norm),
        grid=(s // tm,),
        in_specs=[
            pl.BlockSpec((tm, d), lambda i: (i, 0)),
            pl.BlockSpec((tm, d), lambda i: (i, 0)),
            pl.BlockSpec((None, d, d), lambda i: (layer, 0, 0)),
            pl.BlockSpec((1, d), lambda i: (0, 0)),
        ],
        out_specs=pl.BlockSpec((tm, d), lambda i: (i, 0)),
        out_shape=jax.ShapeDtypeStruct((s, d), F32),
        compiler_params=_params(("parallel",), 48),
        name="outproj",
    )(x2, y, w_out, final_w)


def kernel(x, positions, norm_w, w_in, b_mgate, conv_w, conv_b, m_norm_w, r_norm_w,
           w_proj_m, w_proj_r, b_mix, w_out, final_norm_w):
    batch, s, d = x.shape
    assert batch == 1, "kernel is written for BATCH == 1"
    depth = norm_w.shape[0]
    dqk_m, dv_m = d // 8, d // 4
    dqk_r, dv_r = d // 16, d // 8
    wm_qk, wm_v = NH_M * dqk_m, NH_M * dv_m
    wr_qk, wr_v = NH_R * dqk_r, NH_R * dv_r
    n_gate = N_GATE_TYPES * NH_M
    col_k_m = wm_qk
    col_v_m = 2 * wm_qk
    col_z_m = col_v_m + wm_v
    col_o_m = col_z_m + wm_v
    col_gate = col_o_m + wm_v
    col_q_r = col_gate
    col_k_r = col_q_r + wr_qk
    col_v_r = col_k_r + wr_qk
    col_z_r = col_v_r + wr_v
    col_gate_m = col_z_r + wr_v
    col_gate_r = col_gate_m + d

    x2 = x.reshape(s, d)
    log_gamma = jnp.log1p(-jnp.power(2.0, -5.0 - jnp.arange(NH_R, dtype=F32))).reshape(NH_R, 1, 1)
    inv_freq = jnp.power(ROPE_BASE, -jnp.arange(dqk_r // 2, dtype=F32) / (dqk_r // 2))
    freq_full = jnp.concatenate([inv_freq, inv_freq]).reshape(1, dqk_r)
    sign = jnp.concatenate([-jnp.ones((dqk_r // 2,), F32), jnp.ones((dqk_r // 2,), F32)]).reshape(1, dqk_r)
    cos_full, sin_signed = _rope_tables(positions.reshape(s, 1), freq_full, sign)

    w_main = _drop_gate_cols_bf16(w_in, col_gate, n_gate)
    w_proj_m_bf, w_proj_r_bf, w_out_bf = _cast_bf16(w_proj_m), _cast_bf16(w_proj_r), _cast_bf16(w_out)

    for layer in range(depth):
        w_gate = jnp.pad(w_in[layer, :, col_gate:col_gate + n_gate], ((0, 0), (0, LANE - n_gate))).astype(BF16)
        b_gate = jnp.pad(b_mgate[layer], (0, LANE - n_gate)).reshape(1, LANE)

        proj, gates = _inproj(x2, norm_w[layer].reshape(1, d), w_main, layer, w_gate, b_gate)

        p, pm, pt = _gate_prep(gates, CHUNK_M)

        cw, cb = conv_w[layer], conv_b[layer].reshape(1, 2 * wm_qk)
        q_m = _conv_silu(proj, cw, cb, col0=0, width=wm_qk, scale=dqk_m ** -0.5, transpose_out=False)
        kt_m = _conv_silu(proj, cw, cb, col0=col_k_m, width=wm_qk, scale=1.0, transpose_out=True)
        hmf, hmb = _mlstm_scan(q_m, kt_m, proj, p, pm, pt, dk=dqk_m, dv=dv_m, v_col0=col_v_m, chunk=CHUNK_M)

        q_r, kt_r = _rope(proj, cos_full, sin_signed, dk=dqk_r, q_col0=col_q_r, k_col0=col_k_r)
        hrf, hrb = _retention_scan(q_r, kt_r, proj, log_gamma, dk=dqk_r, dv=dv_r, v_col0=col_v_r, chunk=CHUNK_R)

        y = _merge(hmf, hmb, hrf, hrb, proj, m_norm_w[layer].reshape(1, wm_v), r_norm_w[layer].reshape(1, wr_v),
                   w_proj_m_bf, w_proj_r_bf, layer, b_mix[layer].reshape(1, 2 * d),
                   om_col0=col_o_m, zm_col0=col_z_m, zr_col0=col_z_r, gm_col0=col_gate_m, gr_col0=col_gate_r)
        x2 = _outproj(x2, y, w_out_bf, layer, final_norm_w.reshape(1, d),
                      final_norm=(layer == depth - 1))
    return x2.reshape(batch, s, d)
```

```python
import functools

import jax
import jax.numpy as jnp
from jax import lax
from jax.experimental import pallas as pl
from jax.experimental.pallas import tpu as pltpu

NH_M = 4
NH_R = 8
N_GATE_TYPES = 4
CONV_W = 5
ROPE_BASE = 10000.0
EPS = 1e-6

LANE = 128
BF16_ROWS = 16
MIB = 1024 * 1024

CHUNK_M = 256
CHUNK_R = 256
HEADS_PER_STEP_R = 4

F32 = jnp.float32
BF16 = jnp.bfloat16


def _params(semantics, vmem_mib):
    return pltpu.CompilerParams(dimension_semantics=semantics, vmem_limit_bytes=vmem_mib * MIB)


def _sigmoid(x):
    return 0.5 * jnp.tanh(0.5 * x) + 0.5


def _silu(x):
    half = 0.5 * x
    return half + half * jnp.tanh(half)


def _lanes(x, width):
    return x if width == LANE else jnp.concatenate([x] * (width // LANE), axis=1)


def _dot(a, b):
    return jnp.dot(a, b, preferred_element_type=F32)


def _cast_kernel(x_ref, o_ref):
    o_ref[...] = x_ref[...].astype(o_ref.dtype)


def _cast_bf16(w, *, tr=512):
    depth, rows, cols = w.shape
    spec = pl.BlockSpec((None, tr, cols), lambda l, i: (l, i, 0))
    return pl.pallas_call(
        _cast_kernel,
        grid=(depth, rows // tr),
        in_specs=[spec],
        out_specs=spec,
        out_shape=jax.ShapeDtypeStruct(w.shape, BF16),
        compiler_params=_params(("parallel", "parallel"), 32),
        name="cast_bf16",
    )(w)


def _drop_gate_cols_kernel(a_ref, b_ref, o_ref, g_ref, *, n_plain, shift):
    j = pl.program_id(2)

    @pl.when(j < n_plain)
    def _():
        o_ref[...] = a_ref[...].T.astype(o_ref.dtype)

    @pl.when(j >= n_plain)
    def _():
        rows = jnp.concatenate([a_ref[shift:, :], b_ref[...]], axis=0)
        o_ref[...] = rows.T.astype(o_ref.dtype)

    @pl.when(j == n_plain)
    def _():
        g = a_ref[0:LANE, :].T
        lane = lax.broadcasted_iota(jnp.int32, g.shape, 1)
        g_ref[...] = jnp.where(lane < shift, g, 0.0).astype(g_ref.dtype)


def _drop_gate_cols_bf16(w_in, gate_col0, n_gate, *, tr=512, tn=1024):
    depth, rows, cols = w_in.shape
    n_out = cols - n_gate
    w_t = jnp.transpose(w_in, (0, 2, 1))
    return pl.pallas_call(
        functools.partial(_drop_gate_cols_kernel, n_plain=gate_col0 // tn, shift=n_gate),
        grid=(depth, rows // tr, n_out // tn),
        in_specs=[
            pl.BlockSpec((None, tn, tr), lambda l, i, j: (l, j, i)),
            pl.BlockSpec((None, n_gate, tr), lambda l, i, j: (l, (j + 1) * (tn // n_gate), i)),
        ],
        out_specs=[
            pl.BlockSpec((None, tr, tn), lambda l, i, j: (l, i, j)),
            pl.BlockSpec((None, tr, LANE), lambda l, i, j: (l, i, 0)),
        ],
        out_shape=[jax.ShapeDtypeStruct((depth, rows, n_out), BF16), jax.ShapeDtypeStruct((depth, rows, LANE), BF16)],
        compiler_params=_params(("parallel", "parallel", "arbitrary"), 32),
        name="drop_gate_cols",
    )(w_t, w_t)


def _inproj_kernel(x_ref, nw_ref, w_ref, wg_ref, bg_ref, o_ref, g_ref, h_scr, *, sub_rows):
    tm = x_ref.shape[0]

    @pl.when(pl.program_id(1) == 0)
    def _():
        def body(r, carry):
            rs = pl.ds(pl.multiple_of(r * sub_rows, sub_rows), sub_rows)
            x = x_ref[rs, :]
            ms = jnp.mean(x * x, axis=-1, keepdims=True)
            h = (x * lax.rsqrt(ms + EPS)) * nw_ref[...]
            h_scr[rs, :] = h.astype(BF16)
            return carry

        lax.fori_loop(0, tm // sub_rows, body, 0)
        g_ref[...] = _dot(h_scr[...], wg_ref[...]) + bg_ref[...]

    o_ref[...] = _dot(h_scr[...], w_ref[...]).astype(o_ref.dtype)


def _inproj(x2, norm_w, w_main, layer, w_gate, b_gate, *, tm=1024, tn=2048):
    s, d = x2.shape
    n = w_main.shape[2]
    return pl.pallas_call(
        functools.partial(_inproj_kernel, sub_rows=256),
        grid=(s // tm, n // tn),
        in_specs=[
            pl.BlockSpec((tm, d), lambda i, j: (i, 0)),
            pl.BlockSpec((1, d), lambda i, j: (0, 0)),
            pl.BlockSpec((None, d, tn), lambda i, j: (layer, 0, j)),
            pl.BlockSpec((None, d, LANE), lambda i, j: (layer, 0, 0)),
            pl.BlockSpec((1, LANE), lambda i, j: (0, 0)),
        ],
        out_specs=[
            pl.BlockSpec((tm, tn), lambda i, j: (i, j)),
            pl.BlockSpec((tm, LANE), lambda i, j: (i, 0)),
        ],
        out_shape=[jax.ShapeDtypeStruct((s, n), BF16), jax.ShapeDtypeStruct((s, LANE), F32)],
        scratch_shapes=[pltpu.VMEM((tm, d), BF16)],
        compiler_params=_params(("parallel", "arbitrary"), 56),
        name="inproj",
    )(x2, norm_w, w_main, w_gate, b_gate)


def _gate_prep_kernel(g_ref, p_ref, pm_ref, pt_ref):
    g = g_ref[...]
    chunk = g.shape[0]
    log_f = jnp.minimum(g, 0.0) - jnp.log1p(jnp.exp(-jnp.abs(g)))
    ri = lax.broadcasted_iota(jnp.int32, (chunk, chunk), 0)
    ci = lax.broadcasted_iota(jnp.int32, (chunk, chunk), 1)
    tril = (ci <= ri).astype(F32)
    triu = (ci >= ri).astype(F32)
    cum_prefix = jnp.dot(tril, log_f, precision=lax.Precision.HIGHEST, preferred_element_type=F32)
    cum_suffix = jnp.dot(triu, log_f, precision=lax.Precision.HIGHEST, preferred_element_type=F32)
    lane = lax.broadcasted_iota(jnp.int32, g.shape, 1)
    row = lax.broadcasted_iota(jnp.int32, g.shape, 0)
    cum = jnp.where(lane < 2 * NH_M, cum_prefix, cum_suffix)
    b = g - pltpu.roll(cum, LANE - NH_M, axis=1)
    p = jnp.where((lane % (2 * NH_M)) < NH_M, b, cum)
    p_ref[...] = p
    pt_ref[...] = p.T[:pt_ref.shape[0], :]

    pm_f = b
    pm_b = b
    step = 1
    while step < chunk:
        pm_f = jnp.maximum(pm_f, jnp.where(row >= step, pltpu.roll(pm_f, step, axis=0), -jnp.inf))
        pm_b = jnp.maximum(pm_b, jnp.where(row < chunk - step, pltpu.roll(pm_b, chunk - step, axis=0), -jnp.inf))
        step *= 2
    pm_ref[...] = jnp.where(lane < 2 * NH_M, pm_f, pm_b)


def _gate_prep(g, chunk):
    s = g.shape[0]
    n_rows = N_GATE_TYPES * NH_M
    spec = pl.BlockSpec((chunk, LANE), lambda c: (c, 0))
    return pl.pallas_call(
        _gate_prep_kernel,
        grid=(s // chunk,),
        in_specs=[spec],
        out_specs=[spec, spec, pl.BlockSpec((n_rows, chunk), lambda c: (0, c))],
        out_shape=[jax.ShapeDtypeStruct((s, LANE), F32)] * 2 + [jax.ShapeDtypeStruct((n_rows, s), F32)],
        compiler_params=_params(("parallel",), 32),
        name="gate_prep",
    )(g)


def _conv_kernel(xp_ref, x_ref, xn_ref, w_ref, b_ref, o_ref, xe_ref, *, scale, transpose_out):
    i = pl.program_id(0)
    tb = x_ref.shape[0]
    halo = CONV_W // 2
    xe_ref[0:BF16_ROWS, :] = jnp.where(i > 0, xp_ref[...].astype(F32), 0.0)
    xe_ref[BF16_ROWS:BF16_ROWS + tb, :] = x_ref[...].astype(F32)
    xe_ref[BF16_ROWS + tb:2 * BF16_ROWS + tb, :] = jnp.where(
        i < pl.num_programs(0) - 1, xn_ref[...].astype(F32), 0.0)
    acc = b_ref[...] + w_ref[0:1, :] * xe_ref[pl.ds(BF16_ROWS - halo, tb), :]
    for k in range(1, CONV_W):
        acc = acc + w_ref[k:k + 1, :] * xe_ref[pl.ds(BF16_ROWS - halo + k, tb), :]
    y = _silu(acc)
    if scale != 1.0:
        y = y * scale
    o_ref[...] = (y.T if transpose_out else y).astype(o_ref.dtype)


def _conv_silu(proj, conv_w, conv_b, *, col0, width, scale, transpose_out, tb=512, tc=512):
    s = proj.shape[0]
    rb = tb // BF16_ROWS
    n_rb = s // BF16_ROWS
    cb0 = col0 // tc
    if transpose_out:
        out_spec = pl.BlockSpec((tc, tb), lambda i, j: (j, i))
        out_shape = jax.ShapeDtypeStruct((width, s), BF16)
    else:
        out_spec = pl.BlockSpec((tb, tc), lambda i, j: (i, j))
        out_shape = jax.ShapeDtypeStruct((s, width), BF16)
    return pl.pallas_call(
        functools.partial(_conv_kernel, scale=scale, transpose_out=transpose_out),
        grid=(s // tb, width // tc),
        in_specs=[
            pl.BlockSpec((BF16_ROWS, tc), lambda i, j: (jnp.maximum(i * rb - 1, 0), cb0 + j)),
            pl.BlockSpec((tb, tc), lambda i, j: (i, cb0 + j)),
            pl.BlockSpec((BF16_ROWS, tc), lambda i, j: (jnp.minimum((i + 1) * rb, n_rb - 1), cb0 + j)),
            pl.BlockSpec((CONV_W, tc), lambda i, j: (0, cb0 + j)),
            pl.BlockSpec((1, tc), lambda i, j: (0, cb0 + j)),
        ],
        out_specs=out_spec,
        out_shape=out_shape,
        scratch_shapes=[pltpu.VMEM((tb + 2 * BF16_ROWS, tc), F32)],
        compiler_params=_params(("parallel", "parallel"), 32),
        name="conv_silu_t" if transpose_out else "conv_silu",
    )(proj, proj, proj, conv_w, conv_b)


def _mlstm_direction(q_ref, kt_ref, v_ref, p_ref, pm_ref, pt_ref, o_ref, c_ref, m_ref, va_ref, *, head, reverse):
    chunk, dv = v_ref.shape
    q = q_ref[...]
    kt = kt_ref[...]
    va_ref[:, 0:dv] = v_ref[...]
    va = va_ref[...]
    lane_b = (2 * NH_M if reverse else 0) + head
    lane_cum = lane_b + NH_M
    cum = jnp.broadcast_to(p_ref[:, lane_cum:lane_cum + 1], (chunk, LANE))
    pmax = jnp.broadcast_to(pm_ref[:, lane_b:lane_b + 1], (chunk, LANE))
    b_row = pt_ref[lane_b:lane_b + 1, :]
    last = 0 if reverse else chunk - 1
    total = cum[last:last + 1, :]
    b_max = pmax[last:last + 1, :]
    m_prev = m_ref[...]

    m_loc = jnp.maximum(pmax, m_prev)
    ri = lax.broadcasted_iota(jnp.int32, (chunk, chunk), 0)
    ci = lax.broadcasted_iota(jnp.int32, (chunk, chunk), 1)
    mask = (ci >= ri) if reverse else (ci <= ri)
    expo = jnp.where(mask, b_row - _lanes(m_loc, chunk), -jnp.inf)
    w = (jnp.exp(expo) * _dot(q, kt)).astype(BF16)
    a = jnp.exp(m_prev - m_loc)
    q_c = _dot(q, c_ref[...].astype(BF16))
    num_den = _dot(w, va) + _lanes(a, dv + LANE) * q_c
    den = num_den[:, dv:dv + LANE]
    inv = 1.0 / jnp.maximum(jnp.abs(den), jnp.exp(-(cum + m_loc)))
    for jb in range(dv // LANE):
        cols = slice(jb * LANE, (jb + 1) * LANE)
        o_ref[:, cols] = (num_den[:, cols] * inv).astype(o_ref.dtype)

    m_in = jnp.maximum(m_prev, b_max)
    s_in = jnp.exp(b_row - _lanes(m_in, chunk))
    decay = jnp.exp(m_prev - m_in)
    kt_in = (kt.astype(F32) * s_in).astype(BF16)
    c_ref[...] = _lanes(decay, dv + LANE) * c_ref[...] + _dot(kt_in, va)
    m_ref[...] = total + m_in


def _mlstm_kernel(qf, ktf, vf, pf, pmf, ptf, qb, ktb, vb, pb, pmb, ptb, of, ob, *scratch, heads, dk, dv):
    fwd_state, bwd_state = scratch[:3 * heads], scratch[3 * heads:]

    @pl.when(pl.program_id(1) == 0)
    def _():
        for hh in range(2 * heads):
            c_ref, m_ref, va_ref = scratch[3 * hh:3 * hh + 3]
            c_ref[...] = jnp.zeros(c_ref.shape, c_ref.dtype)
            m_ref[...] = jnp.zeros(m_ref.shape, m_ref.dtype)
            va_ref[:, dv:dv + LANE] = jnp.ones((va_ref.shape[0], LANE), va_ref.dtype)

    for hh in range(heads):
        qk_cols = pl.ds(hh * dk, dk)
        v_cols = pl.ds(hh * dv, dv)
        _mlstm_direction(qf.at[:, qk_cols], ktf.at[qk_cols, :], vf.at[:, v_cols], pf, pmf, ptf,
                         of.at[:, v_cols], *fwd_state[3 * hh:3 * hh + 3], head=hh, reverse=False)
        _mlstm_direction(qb.at[:, qk_cols], ktb.at[qk_cols, :], vb.at[:, v_cols], pb, pmb, ptb,
                         ob.at[:, v_cols], *bwd_state[3 * hh:3 * hh + 3], head=hh, reverse=True)


def _mlstm_scan(q, kt, proj, p, pm, pt, *, dk, dv, v_col0, chunk):
    heads = NH_M
    s = q.shape[0]
    nc = s // chunk
    v_blk0 = v_col0 // (heads * dv)
    n_row = pt.shape[0]

    def fwd(c):
        return c

    def bwd(c):
        return nc - 1 - c

    in_specs = []
    for cmap in (fwd, bwd):
        in_specs += [
            pl.BlockSpec((chunk, heads * dk), lambda h, c, cmap=cmap: (cmap(c), h)),
            pl.BlockSpec((heads * dk, chunk), lambda h, c, cmap=cmap: (h, cmap(c))),
            pl.BlockSpec((chunk, heads * dv), lambda h, c, cmap=cmap: (cmap(c), v_blk0 + h)),
            pl.BlockSpec((chunk, LANE), lambda h, c, cmap=cmap: (cmap(c), 0)),
            pl.BlockSpec((chunk, LANE), lambda h, c, cmap=cmap: (cmap(c), 0)),
            pl.BlockSpec((n_row, chunk), lambda h, c, cmap=cmap: (0, cmap(c))),
        ]
    out_specs = [
        pl.BlockSpec((chunk, heads * dv), lambda h, c: (c, h)),
        pl.BlockSpec((chunk, heads * dv), lambda h, c: (nc - 1 - c, h)),
    ]
    state = [pltpu.VMEM((dk, dv + LANE), F32), pltpu.VMEM((1, LANE), F32), pltpu.VMEM((chunk, dv + LANE), BF16)]
    return pl.pallas_call(
        functools.partial(_mlstm_kernel, heads=heads, dk=dk, dv=dv),
        grid=(NH_M // heads, nc),
        in_specs=in_specs,
        out_specs=out_specs,
        out_shape=[jax.ShapeDtypeStruct((s, NH_M * dv), BF16)] * 2,
        scratch_shapes=state * (2 * heads),
        compiler_params=_params(("parallel", "arbitrary"), 48),
        name="mlstm_scan",
    )(q, kt, proj, p, pm, pt, q, kt, proj, p, pm, pt)


def _rope_table_kernel(pos_ref, freq_ref, sign_ref, cos_ref, sin_ref):
    angle = pos_ref[...].astype(F32) * freq_ref[...]
    cos_ref[...] = jnp.cos(angle)
    sin_ref[...] = jnp.sin(angle) * sign_ref[...]


def _rope_tables(pos_col, freq_full, sign, *, tb=1024):
    s = pos_col.shape[0]
    dk = freq_full.shape[1]
    return pl.pallas_call(
        _rope_table_kernel,
        grid=(s // tb,),
        in_specs=[
            pl.BlockSpec((tb, 1), lambda i: (i, 0)),
            pl.BlockSpec((1, dk), lambda i: (0, 0)),
            pl.BlockSpec((1, dk), lambda i: (0, 0)),
        ],
        out_specs=[pl.BlockSpec((tb, dk), lambda i: (i, 0))] * 2,
        out_shape=[jax.ShapeDtypeStruct((s, dk), F32)] * 2,
        compiler_params=_params(("parallel",), 32),
        name="rope_tables",
    )(pos_col, freq_full, sign)


def _rope_kernel(q_ref, k_ref, cos_ref, sin_ref, qo_ref, kto_ref, *, q_scale):
    dk = cos_ref.shape[1]
    cos = cos_ref[...]
    sin = sin_ref[...]
    for hh in range(q_ref.shape[1] // dk):
        cols = slice(hh * dk, (hh + 1) * dk)
        q = q_ref[:, cols].astype(F32)
        k = k_ref[:, cols].astype(F32)
        q = (q * cos + pltpu.roll(q, dk // 2, axis=1) * sin) * q_scale
        k = k * cos + pltpu.roll(k, dk // 2, axis=1) * sin
        qo_ref[:, cols] = q.astype(qo_ref.dtype)
        kto_ref[cols, :] = k.T.astype(kto_ref.dtype)


def _rope(proj, cos_full, sin_signed, *, dk, q_col0, k_col0, tb=512):
    s = proj.shape[0]
    width = NH_R * dk
    q_blk0, k_blk0 = q_col0 // width, k_col0 // width
    return pl.pallas_call(
        functools.partial(_rope_kernel, q_scale=dk ** -0.5),
        grid=(s // tb,),
        in_specs=[
            pl.BlockSpec((tb, width), lambda i: (i, q_blk0)),
            pl.BlockSpec((tb, width), lambda i: (i, k_blk0)),
            pl.BlockSpec((tb, dk), lambda i: (i, 0)),
            pl.BlockSpec((tb, dk), lambda i: (i, 0)),
        ],
        out_specs=[
            pl.BlockSpec((tb, width), lambda i: (i, 0)),
            pl.BlockSpec((width, tb), lambda i: (0, i)),
        ],
        out_shape=[jax.ShapeDtypeStruct((s, width), BF16), jax.ShapeDtypeStruct((width, s), BF16)],
        compiler_params=_params(("parallel",), 32),
        name="rope",
    )(proj, proj, cos_full, sin_signed)


def _retention_direction(q_ref, kt_ref, v_ref, o_ref, s_ref, d_ref, qd_ref, kd_ref, cd):
    dv = v_ref.shape[1]
    q = q_ref[...]
    kt = kt_ref[...]
    v = v_ref[...]
    scores = (_dot(q, kt) * d_ref[...]).astype(BF16)
    inter = _dot(q, s_ref[...].astype(BF16))
    o_ref[...] = (_dot(scores, v) + _lanes(qd_ref[...], dv) * inter).astype(o_ref.dtype)
    kt_in = (kt.astype(F32) * kd_ref[...]).astype(BF16)
    s_ref[...] = cd * s_ref[...] + _dot(kt_in, v)


def _retention_kernel(lg_ref, qf, ktf, vf, qb, ktb, vb, of, ob, *scratch, heads, dk, dv):
    chunk = qf.shape[0]
    first = pl.program_id(1) == 0
    for hh in range(heads):
        sf, sb, df, db, qdf, kdf, qdb, kdb = scratch[8 * hh:8 * hh + 8]
        lg = lg_ref[hh]

        @pl.when(first)
        def _():
            sf[...] = jnp.zeros(sf.shape, F32)
            sb[...] = jnp.zeros(sb.shape, F32)
            ri = lax.broadcasted_iota(jnp.int32, (chunk, chunk), 0)
            ci = lax.broadcasted_iota(jnp.int32, (chunk, chunk), 1)
            diff = (ri - ci).astype(F32)
            df[...] = jnp.where(diff >= 0.0, jnp.exp(jnp.where(diff >= 0.0, diff, 0.0) * lg), 0.0)
            db[...] = jnp.where(diff < 0.0, jnp.exp(jnp.where(diff < 0.0, -diff, 0.0) * lg), 0.0)
            row = lax.broadcasted_iota(jnp.int32, (chunk, LANE), 0).astype(F32)
            col = lax.broadcasted_iota(jnp.int32, (1, chunk), 1).astype(F32)
            qdf[...] = jnp.exp((row + 1.0) * lg)
            kdf[...] = jnp.exp((chunk - 1.0 - col) * lg)
            qdb[...] = jnp.exp((chunk - row) * lg)
            kdb[...] = jnp.exp(col * lg)

    for hh in range(heads):
        sf, sb, df, db, qdf, kdf, qdb, kdb = scratch[8 * hh:8 * hh + 8]
        cd = jnp.exp(chunk * lg_ref[hh])
        qk_cols = pl.ds(hh * dk, dk)
        v_cols = pl.ds(hh * dv, dv)
        _retention_direction(qf.at[:, qk_cols], ktf.at[qk_cols, :], vf.at[:, v_cols], of.at[:, v_cols],
                             sf, df, qdf, kdf, cd)
        _retention_direction(qb.at[:, qk_cols], ktb.at[qk_cols, :], vb.at[:, v_cols], ob.at[:, v_cols],
                             sb, db, qdb, kdb, cd)


def _retention_scan(q, kt, proj, log_gamma, *, dk, dv, v_col0, chunk, heads=HEADS_PER_STEP_R):
    s = q.shape[0]
    nc = s // chunk
    v_blk0 = v_col0 // (heads * dv)

    def fwd(c):
        return c

    def bwd(c):
        return nc - 1 - c

    in_specs = [pl.BlockSpec((heads, 1, 1), lambda h, c: (h, 0, 0))]
    for cmap in (fwd, bwd):
        in_specs += [
            pl.BlockSpec((chunk, heads * dk), lambda h, c, cmap=cmap: (cmap(c), h)),
            pl.BlockSpec((heads * dk, chunk), lambda h, c, cmap=cmap: (h, cmap(c))),
            pl.BlockSpec((chunk, heads * dv), lambda h, c, cmap=cmap: (cmap(c), v_blk0 + h)),
        ]
    out_specs = [
        pl.BlockSpec((chunk, heads * dv), lambda h, c: (c, h)),
        pl.BlockSpec((chunk, heads * dv), lambda h, c: (nc - 1 - c, h)),
    ]
    per_head = [
        pltpu.VMEM((dk, dv), F32), pltpu.VMEM((dk, dv), F32),
        pltpu.VMEM((chunk, chunk), F32), pltpu.VMEM((chunk, chunk), F32),
        pltpu.VMEM((chunk, LANE), F32), pltpu.VMEM((1, chunk), F32),
        pltpu.VMEM((chunk, LANE), F32), pltpu.VMEM((1, chunk), F32),
    ]
    return pl.pallas_call(
        functools.partial(_retention_kernel, heads=heads, dk=dk, dv=dv),
        grid=(NH_R // heads, nc),
        in_specs=in_specs,
        out_specs=out_specs,
        out_shape=[jax.ShapeDtypeStruct((s, NH_R * dv), BF16)] * 2,
        scratch_shapes=per_head * heads,
        compiler_params=_params(("parallel", "arbitrary"), 40),
        name="retention_scan",
    )(log_gamma, q, kt, proj, q, kt, proj)


def _head_norm_gate(h, gain_ref, z_ref, u_ref, n_heads):
    width = h.shape[1] // n_heads
    for hh in range(n_heads):
        cols = slice(hh * width, (hh + 1) * width)
        seg = h[:, cols]
        y = seg * lax.rsqrt(jnp.mean(seg * seg, axis=-1, keepdims=True) + EPS) * gain_ref[:, cols]
        z = z_ref[:, cols].astype(F32)
        u_ref[:, cols] = (y * _silu(z)).astype(BF16)


def _merge_kernel(hmf, hmb, om, zm, hrf, hrb, zr, mw, rw, wpm, wpr, gm, gr, bmix, y_ref, um, ur, *, tn):
    d = y_ref.shape[1]
    h_m = _sigmoid(om[...].astype(F32)) * (hmf[...].astype(F32) + hmb[...].astype(F32))
    _head_norm_gate(h_m, mw, zm, um, NH_M)
    _head_norm_gate(hrf[...].astype(F32) + hrb[...].astype(F32), rw, zr, ur, NH_R)
    for nb in range(d // tn):
        cols = slice(nb * tn, (nb + 1) * tn)
        cols_r = slice(d + nb * tn, d + (nb + 1) * tn)
        mix_m = _sigmoid(gm[:, cols].astype(F32) + bmix[:, cols])
        mix_r = _sigmoid(gr[:, cols].astype(F32) + bmix[:, cols_r])
        y = mix_m * _dot(um[...], wpm[:, cols]) + mix_r * _dot(ur[...], wpr[:, cols])
        y_ref[:, cols] = y.astype(y_ref.dtype)


def _merge(hmf, hmb, hrf, hrb, proj, m_norm_w, r_norm_w, w_proj_m, w_proj_r, layer, b_mix, *,
           om_col0, zm_col0, zr_col0, gm_col0, gr_col0, tm=256, tn=512):
    s, d = hmf.shape
    wide = lambda col0: pl.BlockSpec((tm, d), lambda i: (i, col0 // d))
    row_full = pl.BlockSpec((tm, d), lambda i: (i, 0))
    const = lambda shape: pl.BlockSpec(shape, lambda i: (0, 0), pipeline_mode=pl.Buffered(1))
    weight = pl.BlockSpec((None, d, d), lambda i: (layer, 0, 0), pipeline_mode=pl.Buffered(1))
    return pl.pallas_call(
        functools.partial(_merge_kernel, tn=tn),
        grid=(s // tm,),
        in_specs=[
            row_full, row_full, wide(om_col0), wide(zm_col0),
            row_full, row_full, wide(zr_col0),
            const((1, d)), const((1, d)), weight, weight,
            wide(gm_col0), wide(gr_col0), const((1, 2 * d)),
        ],
        out_specs=pl.BlockSpec((tm, d), lambda i: (i, 0)),
        out_shape=jax.ShapeDtypeStruct((s, d), BF16),
        scratch_shapes=[pltpu.VMEM((tm, d), BF16), pltpu.VMEM((tm, d), BF16)],
        compiler_params=_params(("parallel",), 48),
        name="merge",
    )(hmf, hmb, proj, proj, hrf, hrb, proj, m_norm_w, r_norm_w, w_proj_m, w_proj_r, proj, proj, b_mix)


def _outproj_kernel(x_ref, y_ref, w_ref, fw_ref, o_ref, *, final_norm):
    out = x_ref[...] + _dot(y_ref[...], w_ref[...])
    if final_norm:
        ms = jnp.mean(out * out, axis=-1, keepdims=True)
        out = (out * lax.rsqrt(ms + EPS)) * fw_ref[...]
    o_ref[...] = out


def _outproj(x2, y, w_out, layer, final_w, *, final_norm, tm=512):
    s, d = x2.shape
    return pl.pallas_call(
        functools.partial(_outproj_kernel, final_norm=final_norm),
        grid=(s // tm,),
        in_specs=[
            pl.BlockSpec((tm, d), lambda i: (i, 0)),
            pl.BlockSpec((tm, d), lambda i: (i, 0)),
            pl.BlockSpec((None, d, d), lambda i: (layer, 0, 0)),
            pl.BlockSpec((1, d), lambda i: (0, 0)),
        ],
        out_specs=pl.BlockSpec((tm, d), lambda i: (i, 0)),
        out_shape=jax.ShapeDtypeStruct((s, d), F32),
        compiler_params=_params(("parallel",), 48),
        name="outproj",
    )(x2, y, w_out, final_w)


def kernel(x, positions, norm_w, w_in, b_mgate, conv_w, conv_b, m_norm_w, r_norm_w,
           w_proj_m, w_proj_r, b_mix, w_out, final_norm_w):
    batch, s, d = x.shape
    assert batch == 1, "kernel is written for BATCH == 1"
    depth = norm_w.shape[0]
    dqk_m, dv_m = d // 8, d // 4
    dqk_r, dv_r = d // 16, d // 8
    wm_qk, wm_v = NH_M * dqk_m, NH_M * dv_m
    wr_qk, wr_v = NH_R * dqk_r, NH_R * dv_r
    n_gate = N_GATE_TYPES * NH_M
    col_k_m = wm_qk
    col_v_m = 2 * wm_qk
    col_z_m = col_v_m + wm_v
    col_o_m = col_z_m + wm_v
    col_gate = col_o_m + wm_v
    col_q_r = col_gate
    col_k_r = col_q_r + wr_qk
    col_v_r = col_k_r + wr_qk
    col_z_r = col_v_r + wr_v
    col_gate_m = col_z_r + wr_v
    col_gate_r = col_gate_m + d

    x2 = x.reshape(s, d)
    log_gamma = jnp.log1p(-jnp.power(2.0, -5.0 - jnp.arange(NH_R, dtype=F32))).reshape(NH_R, 1, 1)
    inv_freq = jnp.power(ROPE_BASE, -jnp.arange(dqk_r // 2, dtype=F32) / (dqk_r // 2))
    freq_full = jnp.concatenate([inv_freq, inv_freq]).reshape(1, dqk_r)
    sign = jnp.concatenate([-jnp.ones((dqk_r // 2,), F32), jnp.ones((dqk_r // 2,), F32)]).reshape(1, dqk_r)
    cos_full, sin_signed = _rope_tables(positions.reshape(s, 1), freq_full, sign)

    w_main, w_gate = _drop_gate_cols_bf16(w_in, col_gate, n_gate)
    w_proj_m_bf, w_proj_r_bf, w_out_bf = _cast_bf16(w_proj_m), _cast_bf16(w_proj_r), _cast_bf16(w_out)

    for layer in range(depth):
        b_gate = jnp.pad(b_mgate[layer], (0, LANE - n_gate)).reshape(1, LANE)

        proj, gates = _inproj(x2, norm_w[layer].reshape(1, d), w_main, layer, w_gate, b_gate)

        p, pm, pt = _gate_prep(gates, CHUNK_M)

        cw, cb = conv_w[layer], conv_b[layer].reshape(1, 2 * wm_qk)
        q_m = _conv_silu(proj, cw, cb, col0=0, width=wm_qk, scale=dqk_m ** -0.5, transpose_out=False)
        kt_m = _conv_silu(proj, cw, cb, col0=col_k_m, width=wm_qk, scale=1.0, transpose_out=True)
        hmf, hmb = _mlstm_scan(q_m, kt_m, proj, p, pm, pt, dk=dqk_m, dv=dv_m, v_col0=col_v_m, chunk=CHUNK_M)

        q_r, kt_r = _rope(proj, cos_full, sin_signed, dk=dqk_r, q_col0=col_q_r, k_col0=col_k_r)
        hrf, hrb = _retention_scan(q_r, kt_r, proj, log_gamma, dk=dqk_r, dv=dv_r, v_col0=col_v_r, chunk=CHUNK_R)

        y = _merge(hmf, hmb, hrf, hrb, proj, m_norm_w[layer].reshape(1, wm_v), r_norm_w[layer].reshape(1, wr_v),
                   w_proj_m_bf, w_proj_r_bf, layer, b_mix[layer].reshape(1, 2 * d),
                   om_col0=col_o_m, zm_col0=col_z_m, zr_col0=col_z_r, gm_col0=col_gate_m, gr_col0=col_gate_r)
        x2 = _outproj(x2, y, w_out_bf, layer, final_norm_w.reshape(1, d),
                      final_norm=(layer == depth - 1))
    return x2.reshape(batch, s, d)
```

```python
import functools

import jax
import jax.numpy as jnp
from jax import lax
from jax.experimental import pallas as pl
from jax.experimental.pallas import tpu as pltpu

NH_M = 4
NH_R = 8
N_GATE_TYPES = 4
CONV_W = 5
ROPE_BASE = 10000.0
EPS = 1e-6

LANE = 128
BF16_ROWS = 16
MIB = 1024 * 1024

CHUNK_M = 256
CHUNK_R = 256
HEADS_PER_STEP_R = 8

F32 = jnp.float32
BF16 = jnp.bfloat16


def _params(semantics, vmem_mib):
    return pltpu.CompilerParams(dimension_semantics=semantics, vmem_limit_bytes=vmem_mib * MIB)


def _sigmoid(x):
    return 0.5 * jnp.tanh(0.5 * x) + 0.5


def _silu(x):
    half = 0.5 * x
    return half + half * jnp.tanh(half)


def _lanes(x, width):
    return x if width == LANE else jnp.concatenate([x] * (width // LANE), axis=1)


def _dot(a, b):
    return jnp.dot(a, b, preferred_element_type=F32)


def _cast_kernel(x_ref, o_ref):
    o_ref[...] = x_ref[...].astype(o_ref.dtype)


def _cast_bf16(w, *, tr=512):
    depth, rows, cols = w.shape
    spec = pl.BlockSpec((None, tr, cols), lambda l, i: (l, i, 0))
    return pl.pallas_call(
        _cast_kernel,
        grid=(depth, rows // tr),
        in_specs=[spec],
        out_specs=spec,
        out_shape=jax.ShapeDtypeStruct(w.shape, BF16),
        compiler_params=_params(("parallel", "parallel"), 32),
        name="cast_bf16",
    )(w)


def _drop_gate_cols_kernel(a_ref, b_ref, o_ref, g_ref, *, n_plain, shift):
    j = pl.program_id(2)

    @pl.when(j < n_plain)
    def _():
        o_ref[...] = a_ref[...].T.astype(o_ref.dtype)

    @pl.when(j >= n_plain)
    def _():
        rows = jnp.concatenate([a_ref[shift:, :], b_ref[...]], axis=0)
        o_ref[...] = rows.T.astype(o_ref.dtype)

    @pl.when(j == n_plain)
    def _():
        g = a_ref[0:LANE, :].T
        lane = lax.broadcasted_iota(jnp.int32, g.shape, 1)
        g_ref[...] = jnp.where(lane < shift, g, 0.0).astype(g_ref.dtype)


def _drop_gate_cols_bf16(w_in, gate_col0, n_gate, *, tr=1024, tn=1024):
    depth, rows, cols = w_in.shape
    n_out = cols - n_gate
    w_t = jnp.transpose(w_in, (0, 2, 1))
    return pl.pallas_call(
        functools.partial(_drop_gate_cols_kernel, n_plain=gate_col0 // tn, shift=n_gate),
        grid=(depth, rows // tr, n_out // tn),
        in_specs=[
            pl.BlockSpec((None, tn, tr), lambda l, i, j: (l, j, i)),
            pl.BlockSpec((None, n_gate, tr), lambda l, i, j: (l, (j + 1) * (tn // n_gate), i)),
        ],
        out_specs=[
            pl.BlockSpec((None, tr, tn), lambda l, i, j: (l, i, j)),
            pl.BlockSpec((None, tr, LANE), lambda l, i, j: (l, i, 0)),
        ],
        out_shape=[jax.ShapeDtypeStruct((depth, rows, n_out), BF16), jax.ShapeDtypeStruct((depth, rows, LANE), BF16)],
        compiler_params=_params(("parallel", "parallel", "arbitrary"), 32),
        name="drop_gate_cols",
    )(w_t, w_t)


def _inproj_kernel(x_ref, nw_ref, w_ref, wg_ref, bg_ref, o_ref, g_ref, h_scr, *, sub_rows):
    tm = x_ref.shape[0]

    @pl.when(pl.program_id(1) == 0)
    def _():
        def body(r, carry):
            rs = pl.ds(pl.multiple_of(r * sub_rows, sub_rows), sub_rows)
            x = x_ref[rs, :]
            ms = jnp.mean(x * x, axis=-1, keepdims=True)
            h = (x * lax.rsqrt(ms + EPS)) * nw_ref[...]
            h_scr[rs, :] = h.astype(BF16)
            return carry

        lax.fori_loop(0, tm // sub_rows, body, 0)
        g_ref[...] = _dot(h_scr[...], wg_ref[...]) + bg_ref[...]

    o_ref[...] = _dot(h_scr[...], w_ref[...]).astype(o_ref.dtype)


def _inproj(x2, norm_w, w_main, layer, w_gate, b_gate, *, tm=1024, tn=2048):
    s, d = x2.shape
    n = w_main.shape[2]
    return pl.pallas_call(
        functools.partial(_inproj_kernel, sub_rows=256),
        grid=(s // tm, n // tn),
        in_specs=[
            pl.BlockSpec((tm, d), lambda i, j: (i, 0)),
            pl.BlockSpec((1, d), lambda i, j: (0, 0)),
            pl.BlockSpec((None, d, tn), lambda i, j: (layer, 0, j)),
            pl.BlockSpec((None, d, LANE), lambda i, j: (layer, 0, 0)),
            pl.BlockSpec((1, LANE), lambda i, j: (0, 0)),
        ],
        out_specs=[
            pl.BlockSpec((tm, tn), lambda i, j: (i, j)),
            pl.BlockSpec((tm, LANE), lambda i, j: (i, 0)),
        ],
        out_shape=[jax.ShapeDtypeStruct((s, n), BF16), jax.ShapeDtypeStruct((s, LANE), F32)],
        scratch_shapes=[pltpu.VMEM((tm, d), BF16)],
        compiler_params=_params(("parallel", "arbitrary"), 56),
        name="inproj",
    )(x2, norm_w, w_main, w_gate, b_gate)


def _gate_prep_kernel(g_ref, p_ref, pm_ref, pt_ref):
    g = g_ref[...]
    chunk = g.shape[0]
    log_f = jnp.minimum(g, 0.0) - jnp.log1p(jnp.exp(-jnp.abs(g)))
    ri = lax.broadcasted_iota(jnp.int32, (chunk, chunk), 0)
    ci = lax.broadcasted_iota(jnp.int32, (chunk, chunk), 1)
    tril = (ci <= ri).astype(F32)
    triu = (ci >= ri).astype(F32)
    cum_prefix = jnp.dot(tril, log_f, precision=lax.Precision.HIGHEST, preferred_element_type=F32)
    cum_suffix = jnp.dot(triu, log_f, precision=lax.Precision.HIGHEST, preferred_element_type=F32)
    lane = lax.broadcasted_iota(jnp.int32, g.shape, 1)
    row = lax.broadcasted_iota(jnp.int32, g.shape, 0)
    cum = jnp.where(lane < 2 * NH_M, cum_prefix, cum_suffix)
    b = g - pltpu.roll(cum, LANE - NH_M, axis=1)
    p = jnp.where((lane % (2 * NH_M)) < NH_M, b, cum)
    p_ref[...] = p
    pt_ref[...] = p.T[:pt_ref.shape[0], :]

    pm_f = b
    pm_b = b
    step = 1
    while step < chunk:
        pm_f = jnp.maximum(pm_f, jnp.where(row >= step, pltpu.roll(pm_f, step, axis=0), -jnp.inf))
        pm_b = jnp.maximum(pm_b, jnp.where(row < chunk - step, pltpu.roll(pm_b, chunk - step, axis=0), -jnp.inf))
        step *= 2
    pm_ref[...] = jnp.where(lane < 2 * NH_M, pm_f, pm_b)


def _gate_prep(g, chunk):
    s = g.shape[0]
    n_rows = N_GATE_TYPES * NH_M
    spec = pl.BlockSpec((chunk, LANE), lambda c: (c, 0))
    return pl.pallas_call(
        _gate_prep_kernel,
        grid=(s // chunk,),
        in_specs=[spec],
        out_specs=[spec, spec, pl.BlockSpec((n_rows, chunk), lambda c: (0, c))],
        out_shape=[jax.ShapeDtypeStruct((s, LANE), F32)] * 2 + [jax.ShapeDtypeStruct((n_rows, s), F32)],
        compiler_params=_params(("parallel",), 32),
        name="gate_prep",
    )(g)


def _conv_kernel(xp_ref, x_ref, xn_ref, w_ref, b_ref, o_ref, xe_ref, *, scale, transpose_out):
    i = pl.program_id(0)
    tb = x_ref.shape[0]
    halo = CONV_W // 2
    xe_ref[0:BF16_ROWS, :] = jnp.where(i > 0, xp_ref[...].astype(F32), 0.0)
    xe_ref[BF16_ROWS:BF16_ROWS + tb, :] = x_ref[...].astype(F32)
    xe_ref[BF16_ROWS + tb:2 * BF16_ROWS + tb, :] = jnp.where(
        i < pl.num_programs(0) - 1, xn_ref[...].astype(F32), 0.0)
    acc = b_ref[...] + w_ref[0:1, :] * xe_ref[pl.ds(BF16_ROWS - halo, tb), :]
    for k in range(1, CONV_W):
        acc = acc + w_ref[k:k + 1, :] * xe_ref[pl.ds(BF16_ROWS - halo + k, tb), :]
    y = _silu(acc)
    if scale != 1.0:
        y = y * scale
    o_ref[...] = (y.T if transpose_out else y).astype(o_ref.dtype)


def _conv_silu(proj, conv_w, conv_b, *, col0, width, scale, transpose_out, tb=1024, tc=512):
    s = proj.shape[0]
    rb = tb // BF16_ROWS
    n_rb = s // BF16_ROWS
    cb0 = col0 // tc
    if transpose_out:
        out_spec = pl.BlockSpec((tc, tb), lambda i, j: (j, i))
        out_shape = jax.ShapeDtypeStruct((width, s), BF16)
    else:
        out_spec = pl.BlockSpec((tb, tc), lambda i, j: (i, j))
        out_shape = jax.ShapeDtypeStruct((s, width), BF16)
    return pl.pallas_call(
        functools.partial(_conv_kernel, scale=scale, transpose_out=transpose_out),
        grid=(s // tb, width // tc),
        in_specs=[
            pl.BlockSpec((BF16_ROWS, tc), lambda i, j: (jnp.maximum(i * rb - 1, 0), cb0 + j)),
            pl.BlockSpec((tb, tc), lambda i, j: (i, cb0 + j)),
            pl.BlockSpec((BF16_ROWS, tc), lambda i, j: (jnp.minimum((i + 1) * rb, n_rb - 1), cb0 + j)),
            pl.BlockSpec((CONV_W, tc), lambda i, j: (0, cb0 + j)),
            pl.BlockSpec((1, tc), lambda i, j: (0, cb0 + j)),
        ],
        out_specs=out_spec,
        out_shape=out_shape,
        scratch_shapes=[pltpu.VMEM((tb + 2 * BF16_ROWS, tc), F32)],
        compiler_params=_params(("parallel", "parallel"), 32),
        name="conv_silu_t" if transpose_out else "conv_silu",
    )(proj, proj, proj, conv_w, conv_b)


def _mlstm_direction(q_ref, kt_ref, v_ref, p_ref, pm_ref, pt_ref, o_ref, c_ref, m_ref, va_ref, *, head, reverse):
    chunk, dv = v_ref.shape
    q = q_ref[...]
    kt = kt_ref[...]
    va_ref[:, 0:dv] = v_ref[...]
    va = va_ref[...]
    lane_b = (2 * NH_M if reverse else 0) + head
    lane_cum = lane_b + NH_M
    cum = jnp.broadcast_to(p_ref[:, lane_cum:lane_cum + 1], (chunk, LANE))
    pmax = jnp.broadcast_to(pm_ref[:, lane_b:lane_b + 1], (chunk, LANE))
    b_row = pt_ref[lane_b:lane_b + 1, :]
    last = 0 if reverse else chunk - 1
    total = cum[last:last + 1, :]
    b_max = pmax[last:last + 1, :]
    m_prev = m_ref[...]

    m_loc = jnp.maximum(pmax, m_prev)
    ri = lax.broadcasted_iota(jnp.int32, (chunk, chunk), 0)
    ci = lax.broadcasted_iota(jnp.int32, (chunk, chunk), 1)
    mask = (ci >= ri) if reverse else (ci <= ri)
    expo = jnp.where(mask, b_row - _lanes(m_loc, chunk), -jnp.inf)
    w = (jnp.exp(expo) * _dot(q, kt)).astype(BF16)
    a = jnp.exp(m_prev - m_loc)
    q_c = _dot(q, c_ref[...].astype(BF16))
    num_den = _dot(w, va) + _lanes(a, dv + LANE) * q_c
    den = num_den[:, dv:dv + LANE]
    inv = 1.0 / jnp.maximum(jnp.abs(den), jnp.exp(-(cum + m_loc)))
    for jb in range(dv // LANE):
        cols = slice(jb * LANE, (jb + 1) * LANE)
        o_ref[:, cols] = (num_den[:, cols] * inv).astype(o_ref.dtype)

    m_in = jnp.maximum(m_prev, b_max)
    s_in = jnp.exp(b_row - _lanes(m_in, chunk))
    decay = jnp.exp(m_prev - m_in)
    kt_in = (kt.astype(F32) * s_in).astype(BF16)
    c_ref[...] = _lanes(decay, dv + LANE) * c_ref[...] + _dot(kt_in, va)
    m_ref[...] = total + m_in


def _mlstm_kernel(qf, ktf, vf, pf, pmf, ptf, qb, ktb, vb, pb, pmb, ptb, of, ob, *scratch, heads, dk, dv):
    fwd_state, bwd_state = scratch[:3 * heads], scratch[3 * heads:]

    @pl.when(pl.program_id(1) == 0)
    def _():
        for hh in range(2 * heads):
            c_ref, m_ref, va_ref = scratch[3 * hh:3 * hh + 3]
            c_ref[...] = jnp.zeros(c_ref.shape, c_ref.dtype)
            m_ref[...] = jnp.zeros(m_ref.shape, m_ref.dtype)
            va_ref[:, dv:dv + LANE] = jnp.ones((va_ref.shape[0], LANE), va_ref.dtype)

    for hh in range(heads):
        qk_cols = pl.ds(hh * dk, dk)
        v_cols = pl.ds(hh * dv, dv)
        _mlstm_direction(qf.at[:, qk_cols], ktf.at[qk_cols, :], vf.at[:, v_cols], pf, pmf, ptf,
                         of.at[:, v_cols], *fwd_state[3 * hh:3 * hh + 3], head=hh, reverse=False)
        _mlstm_direction(qb.at[:, qk_cols], ktb.at[qk_cols, :], vb.at[:, v_cols], pb, pmb, ptb,
                         ob.at[:, v_cols], *bwd_state[3 * hh:3 * hh + 3], head=hh, reverse=True)


def _mlstm_scan(q, kt, proj, p, pm, pt, *, dk, dv, v_col0, chunk):
    heads = NH_M
    s = q.shape[0]
    nc = s // chunk
    v_blk0 = v_col0 // (heads * dv)
    n_row = pt.shape[0]

    def fwd(c):
        return c

    def bwd(c):
        return nc - 1 - c

    in_specs = []
    for cmap in (fwd, bwd):
        in_specs += [
            pl.BlockSpec((chunk, heads * dk), lambda h, c, cmap=cmap: (cmap(c), h)),
            pl.BlockSpec((heads * dk, chunk), lambda h, c, cmap=cmap: (h, cmap(c))),
            pl.BlockSpec((chunk, heads * dv), lambda h, c, cmap=cmap: (cmap(c), v_blk0 + h)),
            pl.BlockSpec((chunk, LANE), lambda h, c, cmap=cmap: (cmap(c), 0)),
            pl.BlockSpec((chunk, LANE), lambda h, c, cmap=cmap: (cmap(c), 0)),
            pl.BlockSpec((n_row, chunk), lambda h, c, cmap=cmap: (0, cmap(c))),
        ]
    out_specs = [
        pl.BlockSpec((chunk, heads * dv), lambda h, c: (c, h)),
        pl.BlockSpec((chunk, heads * dv), lambda h, c: (nc - 1 - c, h)),
    ]
    state = [pltpu.VMEM((dk, dv + LANE), F32), pltpu.VMEM((1, LANE), F32), pltpu.VMEM((chunk, dv + LANE), BF16)]
    return pl.pallas_call(
        functools.partial(_mlstm_kernel, heads=heads, dk=dk, dv=dv),
        grid=(NH_M // heads, nc),
        in_specs=in_specs,
        out_specs=out_specs,
        out_shape=[jax.ShapeDtypeStruct((s, NH_M * dv), BF16)] * 2,
        scratch_shapes=state * (2 * heads),
        compiler_params=_params(("parallel", "arbitrary"), 48),
        name="mlstm_scan",
    )(q, kt, proj, p, pm, pt, q, kt, proj, p, pm, pt)


def _rope_table_kernel(pos_ref, freq_ref, sign_ref, cos_ref, sin_ref):
    angle = pos_ref[...].astype(F32) * freq_ref[...]
    cos_ref[...] = jnp.cos(angle)
    sin_ref[...] = jnp.sin(angle) * sign_ref[...]


def _rope_tables(pos_col, freq_full, sign, *, tb=1024):
    s = pos_col.shape[0]
    dk = freq_full.shape[1]
    return pl.pallas_call(
        _rope_table_kernel,
        grid=(s // tb,),
        in_specs=[
            pl.BlockSpec((tb, 1), lambda i: (i, 0)),
            pl.BlockSpec((1, dk), lambda i: (0, 0)),
            pl.BlockSpec((1, dk), lambda i: (0, 0)),
        ],
        out_specs=[pl.BlockSpec((tb, dk), lambda i: (i, 0))] * 2,
        out_shape=[jax.ShapeDtypeStruct((s, dk), F32)] * 2,
        compiler_params=_params(("parallel",), 32),
        name="rope_tables",
    )(pos_col, freq_full, sign)


def _rope_kernel(q_ref, k_ref, cos_ref, sin_ref, qo_ref, kto_ref, *, q_scale):
    dk = cos_ref.shape[1]
    cos = cos_ref[...]
    sin = sin_ref[...]
    for hh in range(q_ref.shape[1] // dk):
        cols = slice(hh * dk, (hh + 1) * dk)
        q = q_ref[:, cols].astype(F32)
        k = k_ref[:, cols].astype(F32)
        q = (q * cos + pltpu.roll(q, dk // 2, axis=1) * sin) * q_scale
        k = k * cos + pltpu.roll(k, dk // 2, axis=1) * sin
        qo_ref[:, cols] = q.astype(qo_ref.dtype)
        kto_ref[cols, :] = k.T.astype(kto_ref.dtype)


def _rope(proj, cos_full, sin_signed, *, dk, q_col0, k_col0, tb=512):
    s = proj.shape[0]
    width = NH_R * dk
    q_blk0, k_blk0 = q_col0 // width, k_col0 // width
    return pl.pallas_call(
        functools.partial(_rope_kernel, q_scale=dk ** -0.5),
        grid=(s // tb,),
        in_specs=[
            pl.BlockSpec((tb, width), lambda i: (i, q_blk0)),
            pl.BlockSpec((tb, width), lambda i: (i, k_blk0)),
            pl.BlockSpec((tb, dk), lambda i: (i, 0)),
            pl.BlockSpec((tb, dk), lambda i: (i, 0)),
        ],
        out_specs=[
            pl.BlockSpec((tb, width), lambda i: (i, 0)),
            pl.BlockSpec((width, tb), lambda i: (0, i)),
        ],
        out_shape=[jax.ShapeDtypeStruct((s, width), BF16), jax.ShapeDtypeStruct((width, s), BF16)],
        compiler_params=_params(("parallel",), 32),
        name="rope",
    )(proj, proj, cos_full, sin_signed)


def _retention_direction(q_ref, kt_ref, v_ref, o_ref, s_ref, d_ref, qd_ref, kd_ref, cd):
    dv = v_ref.shape[1]
    q = q_ref[...]
    kt = kt_ref[...]
    v = v_ref[...]
    scores = (_dot(q, kt) * d_ref[...]).astype(BF16)
    inter = _dot(q, s_ref[...].astype(BF16))
    o_ref[...] = (_dot(scores, v) + _lanes(qd_ref[...], dv) * inter).astype(o_ref.dtype)
    kt_in = (kt.astype(F32) * kd_ref[...]).astype(BF16)
    s_ref[...] = cd * s_ref[...] + _dot(kt_in, v)


def _retention_kernel(lg_ref, qf, ktf, vf, qb, ktb, vb, of, ob, *scratch, heads, dk, dv):
    chunk = qf.shape[0]
    first = pl.program_id(1) == 0
    for hh in range(heads):
        sf, sb, df, db, qdf, kdf, qdb, kdb = scratch[8 * hh:8 * hh + 8]
        lg = lg_ref[hh]

        @pl.when(first)
        def _():
            sf[...] = jnp.zeros(sf.shape, F32)
            sb[...] = jnp.zeros(sb.shape, F32)
            ri = lax.broadcasted_iota(jnp.int32, (chunk, chunk), 0)
            ci = lax.broadcasted_iota(jnp.int32, (chunk, chunk), 1)
            diff = (ri - ci).astype(F32)
            df[...] = jnp.where(diff >= 0.0, jnp.exp(jnp.where(diff >= 0.0, diff, 0.0) * lg), 0.0)
            db[...] = jnp.where(diff < 0.0, jnp.exp(jnp.where(diff < 0.0, -diff, 0.0) * lg), 0.0)
            row = lax.broadcasted_iota(jnp.int32, (chunk, LANE), 0).astype(F32)
            col = lax.broadcasted_iota(jnp.int32, (1, chunk), 1).astype(F32)
            qdf[...] = jnp.exp((row + 1.0) * lg)
            kdf[...] = jnp.exp((chunk - 1.0 - col) * lg)
            qdb[...] = jnp.exp((chunk - row) * lg)
            kdb[...] = jnp.exp(col * lg)

    for hh in range(heads):
        sf, sb, df, db, qdf, kdf, qdb, kdb = scratch[8 * hh:8 * hh + 8]
        cd = jnp.exp(chunk * lg_ref[hh])
        qk_cols = pl.ds(hh * dk, dk)
        v_cols = pl.ds(hh * dv, dv)
        _retention_direction(qf.at[:, qk_cols], ktf.at[qk_cols, :], vf.at[:, v_cols], of.at[:, v_cols],
                             sf, df, qdf, kdf, cd)
        _retention_direction(qb.at[:, qk_cols], ktb.at[qk_cols, :], vb.at[:, v_cols], ob.at[:, v_cols],
                             sb, db, qdb, kdb, cd)


def _retention_scan(q, kt, proj, log_gamma, *, dk, dv, v_col0, chunk, heads=HEADS_PER_STEP_R):
    s = q.shape[0]
    nc = s // chunk
    v_blk0 = v_col0 // (heads * dv)

    def fwd(c):
        return c

    def bwd(c):
        return nc - 1 - c

    in_specs = [pl.BlockSpec((heads, 1, 1), lambda h, c: (h, 0, 0))]
    for cmap in (fwd, bwd):
        in_specs += [
            pl.BlockSpec((chunk, heads * dk), lambda h, c, cmap=cmap: (cmap(c), h)),
            pl.BlockSpec((heads * dk, chunk), lambda h, c, cmap=cmap: (h, cmap(c))),
            pl.BlockSpec((chunk, heads * dv), lambda h, c, cmap=cmap: (cmap(c), v_blk0 + h)),
        ]
    out_specs = [
        pl.BlockSpec((chunk, heads * dv), lambda h, c: (c, h)),
        pl.BlockSpec((chunk, heads * dv), lambda h, c: (nc - 1 - c, h)),
    ]
    per_head = [
        pltpu.VMEM((dk, dv), F32), pltpu.VMEM((dk, dv), F32),
        pltpu.VMEM((chunk, chunk), F32), pltpu.VMEM((chunk, chunk), F32),
        pltpu.VMEM((chunk, LANE), F32), pltpu.VMEM((1, chunk), F32),
        pltpu.VMEM((chunk, LANE), F32), pltpu.VMEM((1, chunk), F32),
    ]
    return pl.pallas_call(
        functools.partial(_retention_kernel, heads=heads, dk=dk, dv=dv),
        grid=(NH_R // heads, nc),
        in_specs=in_specs,
        out_specs=out_specs,
        out_shape=[jax.ShapeDtypeStruct((s, NH_R * dv), BF16)] * 2,
        scratch_shapes=per_head * heads,
        compiler_params=_params(("parallel", "arbitrary"), 40),
        name="retention_scan",
    )(log_gamma, q, kt, proj, q, kt, proj)


def _head_norm_gate(h, gain_ref, z_ref, u_ref, n_heads):
    width = h.shape[1] // n_heads
    for hh in range(n_heads):
        cols = slice(hh * width, (hh + 1) * width)
        seg = h[:, cols]
        y = seg * lax.rsqrt(jnp.mean(seg * seg, axis=-1, keepdims=True) + EPS) * gain_ref[:, cols]
        z = z_ref[:, cols].astype(F32)
        u_ref[:, cols] = (y * _silu(z)).astype(BF16)


def _merge_kernel(hmf, hmb, om, zm, hrf, hrb, zr, mw, rw, wpm, wpr, gm, gr, bmix, y_ref, um, ur, *, tn):
    d = y_ref.shape[1]
    h_m = _sigmoid(om[...].astype(F32)) * (hmf[...].astype(F32) + hmb[...].astype(F32))
    _head_norm_gate(h_m, mw, zm, um, NH_M)
    _head_norm_gate(hrf[...].astype(F32) + hrb[...].astype(F32), rw, zr, ur, NH_R)
    for nb in range(d // tn):
        cols = slice(nb * tn, (nb + 1) * tn)
        cols_r = slice(d + nb * tn, d + (nb + 1) * tn)
        mix_m = _sigmoid(gm[:, cols].astype(F32) + bmix[:, cols])
        mix_r = _sigmoid(gr[:, cols].astype(F32) + bmix[:, cols_r])
        y = mix_m * _dot(um[...], wpm[:, cols]) + mix_r * _dot(ur[...], wpr[:, cols])
        y_ref[:, cols] = y.astype(y_ref.dtype)


def _merge(hmf, hmb, hrf, hrb, proj, m_norm_w, r_norm_w, w_proj_m, w_proj_r, layer, b_mix, *,
           om_col0, zm_col0, zr_col0, gm_col0, gr_col0, tm=256, tn=512):
    s, d = hmf.shape
    wide = lambda col0: pl.BlockSpec((tm, d), lambda i: (i, col0 // d))
    row_full = pl.BlockSpec((tm, d), lambda i: (i, 0))
    const = lambda shape: pl.BlockSpec(shape, lambda i: (0, 0), pipeline_mode=pl.Buffered(1))
    weight = pl.BlockSpec((None, d, d), lambda i: (layer, 0, 0), pipeline_mode=pl.Buffered(1))
    return pl.pallas_call(
        functools.partial(_merge_kernel, tn=tn),
        grid=(s // tm,),
        in_specs=[
            row_full, row_full, wide(om_col0), wide(zm_col0),
            row_full, row_full, wide(zr_col0),
            const((1, d)), const((1, d)), weight, weight,
            wide(gm_col0), wide(gr_col0), const((1, 2 * d)),
        ],
        out_specs=pl.BlockSpec((tm, d), lambda i: (i, 0)),
        out_shape=jax.ShapeDtypeStruct((s, d), BF16),
        scratch_shapes=[pltpu.VMEM((tm, d), BF16), pltpu.VMEM((tm, d), BF16)],
        compiler_params=_params(("parallel",), 48),
        name="merge",
    )(hmf, hmb, proj, proj, hrf, hrb, proj, m_norm_w, r_norm_w, w_proj_m, w_proj_r, proj, proj, b_mix)


def _outproj_kernel(x_ref, y_ref, w_ref, fw_ref, o_ref, *, final_norm):
    out = x_ref[...] + _dot(y_ref[...], w_ref[...])
    if final_norm:
        ms = jnp.mean(out * out, axis=-1, keepdims=True)
        out = (out * lax.rsqrt(ms + EPS)) * fw_ref[...]
    o_ref[...] = out


def _outproj(x2, y, w_out, layer, final_w, *, final_norm, tm=512):
    s, d = x2.shape
    return pl.pallas_call(
        functools.partial(_outproj_kernel, final_norm=final_norm),
        grid=(s // tm,),
        in_specs=[
            pl.BlockSpec((tm, d), lambda i: (i, 0)),
            pl.BlockSpec((tm, d), lambda i: (i, 0)),
            pl.BlockSpec((None, d, d), lambda i: (layer, 0, 0)),
            pl.BlockSpec((1, d), lambda i: (0, 0)),
        ],
        out_specs=pl.BlockSpec((tm, d), lambda i: (i, 0)),
        out_shape=jax.ShapeDtypeStruct((s, d), F32),
        compiler_params=_params(("parallel",), 48),
        name="outproj",
    )(x2, y, w_out, final_w)


def kernel(x, positions, norm_w, w_in, b_mgate, conv_w, conv_b, m_norm_w, r_norm_w,
           w_proj_m, w_proj_r, b_mix, w_out, final_norm_w):
    batch, s, d = x.shape
    assert batch == 1, "kernel is written for BATCH == 1"
    depth = norm_w.shape[0]
    dqk_m, dv_m = d // 8, d // 4
    dqk_r, dv_r = d // 16, d // 8
    wm_qk, wm_v = NH_M * dqk_m, NH_M * dv_m
    wr_qk, wr_v = NH_R * dqk_r, NH_R * dv_r
    n_gate = N_GATE_TYPES * NH_M
    col_k_m = wm_qk
    col_v_m = 2 * wm_qk
    col_z_m = col_v_m + wm_v
    col_o_m = col_z_m + wm_v
    col_gate = col_o_m + wm_v
    col_q_r = col_gate
    col_k_r = col_q_r + wr_qk
    col_v_r = col_k_r + wr_qk
    col_z_r = col_v_r + wr_v
    col_gate_m = col_z_r + wr_v
    col_gate_r = col_gate_m + d

    x2 = x.reshape(s, d)
    log_gamma = jnp.log1p(-jnp.power(2.0, -5.0 - jnp.arange(NH_R, dtype=F32))).reshape(NH_R, 1, 1)
    inv_freq = jnp.power(ROPE_BASE, -jnp.arange(dqk_r // 2, dtype=F32) / (dqk_r // 2))
    freq_full = jnp.concatenate([inv_freq, inv_freq]).reshape(1, dqk_r)
    sign = jnp.concatenate([-jnp.ones((dqk_r // 2,), F32), jnp.ones((dqk_r // 2,), F32)]).reshape(1, dqk_r)
    cos_full, sin_signed = _rope_tables(positions.reshape(s, 1), freq_full, sign)

    w_main, w_gate = _drop_gate_cols_bf16(w_in, col_gate, n_gate)
    w_proj_m_bf, w_proj_r_bf, w_out_bf = _cast_bf16(w_proj_m), _cast_bf16(w_proj_r), _cast_bf16(w_out)

    for layer in range(depth):
        b_gate = jnp.pad(b_mgate[layer], (0, LANE - n_gate)).reshape(1, LANE)

        proj, gates = _inproj(x2, norm_w[layer].reshape(1, d), w_main, layer, w_gate, b_gate)

        p, pm, pt = _gate_prep(gates, CHUNK_M)

        cw, cb = conv_w[layer], conv_b[layer].reshape(1, 2 * wm_qk)
        q_m = _conv_silu(proj, cw, cb, col0=0, width=wm_qk, scale=dqk_m ** -0.5, transpose_out=False)
        kt_m = _conv_silu(proj, cw, cb, col0=col_k_m, width=wm_qk, scale=1.0, transpose_out=True)
        hmf, hmb = _mlstm_scan(q_m, kt_m, proj, p, pm, pt, dk=dqk_m, dv=dv_m, v_col0=col_v_m, chunk=CHUNK_M)

        q_r, kt_r = _rope(proj, cos_full, sin_signed, dk=dqk_r, q_col0=col_q_r, k_col0=col_k_r)
        hrf, hrb = _retention_scan(q_r, kt_r, proj, log_gamma, dk=dqk_r, dv=dv_r, v_col0=col_v_r, chunk=CHUNK_R)

        y = _merge(hmf, hmb, hrf, hrb, proj, m_norm_w[layer].reshape(1, wm_v), r_norm_w[layer].reshape(1, wr_v),
                   w_proj_m_bf, w_proj_r_bf, layer, b_mix[layer].reshape(1, 2 * d),
                   om_col0=col_o_m, zm_col0=col_z_m, zr_col0=col_z_r, gm_col0=col_gate_m, gr_col0=col_gate_r)
        x2 = _outproj(x2, y, w_out_bf, layer, final_norm_w.reshape(1, d),
                      final_norm=(layer == depth - 1))
    return x2.reshape(batch, s, d)
```
